```python
import jax, jax.numpy as jnp
from jax import lax
import numpy as np

D_MODEL = 1024
BATCH = 8
SEQ = 8192
DEPTH = 4

EXPAND = 2
D_BRANCH = EXPAND * D_MODEL
HEAD_DIM = 64
N_HEADS = D_BRANCH // HEAD_DIM
N_MIXERS = 3
N_LAYERS_A = (DEPTH + 2) // 3
N_LAYERS_B = (DEPTH + 1) // 3
N_LAYERS_C = DEPTH // 3
RMS_EPS = 1e-5
BLOCK = 128

POOL_WINDOWS = (2, 4, 8, 16)
N_POOL_GROUPS = len(POOL_WINDOWS)
POOL_GROUP_DIM = D_BRANCH // N_POOL_GROUPS
A_IN_WIDTH = 2 * D_BRANCH

SWA_WINDOW = 128
SWA_KV_HEADS = N_HEADS // 8
SWA_GROUP = N_HEADS // SWA_KV_HEADS
KV_WIDTH = SWA_KV_HEADS * HEAD_DIM
B_IN_WIDTH = 2 * D_BRANCH + 2 * KV_WIDTH

DILATED_PAIRS = ((128, 1), (512, 4), (2048, 16))
N_DIL_GROUPS = len(DILATED_PAIRS)
C_IN_WIDTH = (3 * N_DIL_GROUPS + 1) * D_BRANCH

kernel_name = "hybrid_pool_swa_dilated_gated_trunk"


def _rmsnorm(x, g):
    x32 = x.astype(jnp.float32)
    y = x32 * lax.rsqrt(jnp.mean(x32 * x32, axis=-1, keepdims=True) + RMS_EPS)
    return (y * g.astype(jnp.float32)).astype(x.dtype)


def _banded_attention(q, k, v, max_dist, sink):
    r_n, n, seq_len, hk, grp, hd = q.shape
    n_blk = -(-seq_len // BLOCK)
    pad_end = n_blk * BLOCK - seq_len
    qp = jnp.pad(q, ((0, 0), (0, 0), (0, pad_end), (0, 0), (0, 0), (0, 0)))
    kv_pad = ((0, 0), (0, 0), (BLOCK, pad_end), (0, 0), (0, 0))
    kp = jnp.pad(k, kv_pad)
    vp = jnp.pad(v, kv_pad)
    scale = HEAD_DIM ** -0.5

    def block(idx):
        r = idx // n_blk
        start = (idx % n_blk) * BLOCK
        qr = lax.dynamic_index_in_dim(qp, r, 0, keepdims=False)
        kr = lax.dynamic_index_in_dim(kp, r, 0, keepdims=False)
        vr = lax.dynamic_index_in_dim(vp, r, 0, keepdims=False)
        qb = lax.dynamic_slice_in_dim(qr, start, BLOCK, axis=1).astype(jnp.float32)
        kb = lax.dynamic_slice_in_dim(kr, start, 2 * BLOCK, axis=1).astype(jnp.float32)
        vb = lax.dynamic_slice_in_dim(vr, start, 2 * BLOCK, axis=1).astype(jnp.float32)
        s = jnp.einsum('nqkgd,nskd->nkgqs', qb, kb) * scale
        q_pos = start + jnp.arange(BLOCK)
        k_pos = start - BLOCK + jnp.arange(2 * BLOCK)
        dist = q_pos[:, None] - k_pos[None, :]
        valid = (dist >= 0) & (dist <= max_dist) & (k_pos >= 0)[None, :]
        s = jnp.where(valid, s, -jnp.inf)
        lse = jax.nn.logsumexp(s, axis=-1)
        if sink is not None:
            lse = jnp.logaddexp(lse, sink.astype(jnp.float32)[None, :, :, None])
        p = jnp.exp(s - lse[..., None])
        o = jnp.einsum('nkgqs,nskd->nqkgd', p, vb)
        return o, lse

    o, lse = lax.map(block, jnp.arange(r_n * n_blk))
    o = o.reshape(r_n, n_blk, n, BLOCK, hk, grp, hd)
    o = jnp.transpose(o, (0, 2, 1, 3, 4, 5, 6)).reshape(r_n, n, n_blk * BLOCK, hk, grp, hd)[:, :, :seq_len]
    lse = lse.reshape(r_n, n_blk, n, hk, grp, BLOCK)
    lse = jnp.transpose(lse, (0, 2, 1, 5, 3, 4)).reshape(r_n, n, n_blk * BLOCK, hk, grp)[:, :, :seq_len]
    return o, lse


def _pool_mixer(u, w_group, scale):
    bsz, seq, _ = u.shape
    ug = u.reshape(bsz, seq, N_POOL_GROUPS, POOL_GROUP_DIM)
    c = jnp.cumsum(ug.astype(jnp.float32), axis=1)
    t = jnp.arange(seq)
    pooled = []
    for gi, w in enumerate(POOL_WINDOWS):
        cg = c[:, :, gi]
        shifted = jnp.pad(cg, ((0, 0), (w, 0), (0, 0)))[:, :seq]
        count = jnp.minimum(t + 1, w).astype(jnp.float32)
        pooled.append((cg - shifted) / count[None, :, None])
    d = (jnp.stack(pooled, axis=2) - ug.astype(jnp.float32)).astype(u.dtype)
    y = jnp.einsum('bsgc,gcd->bsgd', d, w_group)
    return y.reshape(bsz, seq, D_BRANCH) * scale


def _swa_mixer(p, sinks):
    bsz, seq, _ = p.shape
    q, k, v, gate = jnp.split(p, [D_BRANCH, D_BRANCH + KV_WIDTH, D_BRANCH + 2 * KV_WIDTH], axis=-1)
    q = q.reshape(1, bsz, seq, SWA_KV_HEADS, SWA_GROUP, HEAD_DIM)
    k = k.reshape(1, bsz, seq, SWA_KV_HEADS, HEAD_DIM)
    v = v.reshape(1, bsz, seq, SWA_KV_HEADS, HEAD_DIM)
    o, _ = _banded_attention(q, k, v, SWA_WINDOW - 1, sinks.reshape(SWA_KV_HEADS, SWA_GROUP))
    return o[0].reshape(bsz, seq, D_BRANCH).astype(p.dtype), gate


def _to_residues(a, d):
    bsz, seq = a.shape[:2]
    a = a.reshape((bsz, seq // d, d) + a.shape[2:])
    return jnp.moveaxis(a, 2, 0)


def _from_residues(a):
    d, bsz, l = a.shape[:3]
    a = jnp.moveaxis(a, 0, 2)
    return a.reshape((bsz, l * d) + a.shape[3:])


def _dilated_mixer(h, w_in):
    bsz, seq, _ = h.shape

    def proj(c):
        return jnp.einsum('bsd,de->bse', h, w_in[:, c * D_BRANCH:(c + 1) * D_BRANCH])

    outs, lses = [], []
    for gi, (window, dil) in enumerate(DILATED_PAIRS):
        q, k, v = (proj(3 * gi + j).reshape(bsz, seq, N_HEADS, HEAD_DIM) for j in range(3))
        o, lse = _banded_attention(_to_residues(q, dil)[..., None, :], _to_residues(k, dil),
                                   _to_residues(v, dil), window // dil, None)
        outs.append(_from_residues(o[..., 0, :]))
        lses.append(_from_residues(lse[..., 0]))
    wts = jax.nn.softmax(jnp.stack(lses, 0), axis=0)
    o = jnp.einsum('gbsh,gbshd->bshd', wts, jnp.stack(outs, 0))
    return o.reshape(bsz, seq, D_BRANCH).astype(h.dtype), proj(3 * N_DIL_GROUPS)


def setup_inputs(seed: int = 0) -> dict:
    key = jax.random.key(seed)
    ks = jax.random.split(key, 11)
    f32 = jnp.float32
    nrm = jax.random.normal
    return {
        "x": nrm(ks[0], (BATCH, SEQ, D_MODEL), f32),
        "norm_g": 1.0 + 0.05 * nrm(ks[1], (DEPTH, D_MODEL), f32),
        "final_g": 1.0 + 0.05 * nrm(ks[2], (D_MODEL,), f32),
        "w_out": nrm(ks[3], (DEPTH, D_BRANCH, D_MODEL), f32) * D_BRANCH ** -0.5,
        "a_w_in": nrm(ks[4], (N_LAYERS_A, D_MODEL, A_IN_WIDTH), f32) * D_MODEL ** -0.5,
        "a_w_group": nrm(ks[5], (N_LAYERS_A, N_POOL_GROUPS, POOL_GROUP_DIM, POOL_GROUP_DIM), f32) * POOL_GROUP_DIM ** -0.5,
        "a_scale": 1.0 + 0.1 * nrm(ks[6], (N_LAYERS_A, D_BRANCH), f32),
        "b_w_in": nrm(ks[7], (N_LAYERS_B, D_MODEL, B_IN_WIDTH), f32) * D_MODEL ** -0.5,
        "b_sinks": 0.5 * nrm(ks[8], (N_LAYERS_B, N_HEADS), f32),
        "c_w_in": nrm(ks[9], (N_LAYERS_C, D_MODEL, C_IN_WIDTH), f32) * D_MODEL ** -0.5,
    }


def reference(x, norm_g, final_g, w_out, a_w_in, a_w_group, a_scale, b_w_in, b_sinks, c_w_in):
    for i in range(DEPTH):
        h = _rmsnorm(x, norm_g[i])
        kind, j = i % N_MIXERS, i // N_MIXERS
        if kind == 0:
            p = jnp.einsum('bsd,de->bse', h, a_w_in[j])
            u, gate = jnp.split(p, 2, axis=-1)
            y = _pool_mixer(u, a_w_group[j], a_scale[j])
        elif kind == 1:
            y, gate = _swa_mixer(jnp.einsum('bsd,de->bse', h, b_w_in[j]), b_sinks[j])
        else:
            y, gate = _dilated_mixer(h, c_w_in[j])
        x = x + jnp.einsum('bse,ed->bsd', y * jax.nn.silu(gate), w_out[i])
    return _rmsnorm(x, final_g)
```

```python
import functools

import jax
import jax.numpy as jnp
from jax import lax
from jax.experimental import pallas as pl
from jax.experimental.pallas import tpu as pltpu

HEAD_DIM = 64
RMS_EPS = 1e-5
BAND = 128
POOL_WINDOWS = (2, 4, 8, 16)
POOL_HALO = 16
SWA_GROUP = 8
DILATIONS = (1, 4, 16)
N_MIXERS = 3

V7X_VMEM_LIMIT_BYTES = 56 * 1024 * 1024
NEG = -1e30

F32 = jnp.float32
BF16 = jnp.bfloat16


def _params(*semantics):
    return pltpu.CompilerParams(dimension_semantics=semantics, vmem_limit_bytes=V7X_VMEM_LIMIT_BYTES)


def _rmsnorm(x, g):
    return x * lax.rsqrt(jnp.mean(x * x, axis=-1, keepdims=True) + RMS_EPS) * g


def _silu(g):
    return g / (1.0 + jnp.exp(-g))


def _band_mask(inclusive_far_edge):
    row = lax.broadcasted_iota(jnp.int32, (BAND, 2 * BAND), 0)
    col = lax.broadcasted_iota(jnp.int32, (BAND, 2 * BAND), 1)
    lo = row if inclusive_far_edge else row + 1
    return (col >= lo) & (col <= row + BAND), col


def _attend(q, k, v, mask, sink):
    s = lax.dot_general(q, k, (((1,), (1,)), ((), ())), preferred_element_type=F32)
    s = jnp.where(mask, s, NEG)
    m = jnp.max(s, axis=-1, keepdims=True)
    if sink is not None:
        m = jnp.maximum(m, sink)
    e = jnp.exp(s - m)
    den = jnp.sum(e, axis=-1, keepdims=True)
    if sink is not None:
        den = den + jnp.exp(sink - m)
    o = jnp.dot(e.astype(BF16), v, preferred_element_type=F32)
    return o / den, m + jnp.log(den)


def _proj_kernel(x_ref, g_ref, w_ref, o_ref, h_ref):
    @pl.when(pl.program_id(1) == 0)
    def _():
        h_ref[...] = _rmsnorm(x_ref[...], g_ref[...]).astype(BF16)

    o_ref[...] = jnp.dot(h_ref[...], w_ref[...], preferred_element_type=F32).astype(o_ref.dtype)


def _proj(x2d, g, w, *, tm, tn):
    t, d = x2d.shape
    e = w.shape[1]
    assert t % tm == 0 and e % tn == 0
    return pl.pallas_call(
        _proj_kernel,
        grid=(t // tm, e // tn),
        in_specs=[
            pl.BlockSpec((tm, d), lambda i, j: (i, 0)),
            pl.BlockSpec((1, d), lambda i, j: (0, 0)),
            pl.BlockSpec((d, tn), lambda i, j: (0, j)),
        ],
        out_specs=pl.BlockSpec((tm, tn), lambda i, j: (i, j)),
        out_shape=jax.ShapeDtypeStruct((t, e), BF16),
        scratch_shapes=[pltpu.VMEM((tm, d), BF16)],
        compiler_params=_params("arbitrary", "arbitrary"),
        name="in_proj",
    )(x2d, g, w)


def _pool_layer_kernel(x_ref, g_ref, win_ref, wg_ref, sc_ref, wout_ref, fg_ref, o_ref, ubuf, zbuf,
                       *, tq, db, final_norm):
    j = pl.program_id(1)
    x = x_ref[...]
    h = _rmsnorm(x, g_ref[...]).astype(BF16)
    p = jnp.dot(h, win_ref[...], preferred_element_type=F32)

    @pl.when(j == 0)
    def _():
        ubuf[0:POOL_HALO, :] = jnp.zeros((POOL_HALO, db), F32)

    @pl.when(j > 0)
    def _():
        ubuf[0:POOL_HALO, :] = ubuf[tq:tq + POOL_HALO, :]

    ubuf[POOL_HALO:POOL_HALO + tq, :] = p[:, :db]

    t_abs = j * tq + lax.broadcasted_iota(jnp.int32, (tq, 1), 0)
    gd = db // len(POOL_WINDOWS)
    for gi, w in enumerate(POOL_WINDOWS):
        cols = slice(gi * gd, (gi + 1) * gd)
        u = ubuf[POOL_HALO:POOL_HALO + tq, cols]
        acc = u
        for k in range(1, w):
            acc = acc + ubuf[POOL_HALO - k:POOL_HALO - k + tq, cols]
        count = jnp.minimum(t_abs + 1, w).astype(F32)
        d = (acc / count - u).astype(BF16)
        y = jnp.dot(d, wg_ref[gi], preferred_element_type=F32) * sc_ref[:, cols]
        gate = p[:, db + gi * gd:db + (gi + 1) * gd]
        zbuf[:, cols] = (y * _silu(gate)).astype(BF16)

    out = x + jnp.dot(zbuf[...], wout_ref[...], preferred_element_type=F32)
    if final_norm:
        out = _rmsnorm(out, fg_ref[...])
    o_ref[...] = out


def _pool_layer(x, g, w_in, w_group, scale, w_out, final_g, *, final_norm, tq):
    b, s, d = x.shape
    db = w_out.shape[0]
    assert s % tq == 0 and tq >= POOL_HALO
    const2 = lambda bi, j: (0, 0)
    return pl.pallas_call(
        functools.partial(_pool_layer_kernel, tq=tq, db=db, final_norm=final_norm),
        grid=(b, s // tq),
        in_specs=[
            pl.BlockSpec((None, tq, d), lambda bi, j: (bi, j, 0)),
            pl.BlockSpec((1, d), const2),
            pl.BlockSpec(w_in.shape, const2),
            pl.BlockSpec(w_group.shape, lambda bi, j: (0, 0, 0)),
            pl.BlockSpec((1, db), const2),
            pl.BlockSpec(w_out.shape, const2),
            pl.BlockSpec((1, d), const2),
        ],
        out_specs=pl.BlockSpec((None, tq, d), lambda bi, j: (bi, j, 0)),
        out_shape=jax.ShapeDtypeStruct(x.shape, F32),
        scratch_shapes=[pltpu.VMEM((POOL_HALO + tq, db), F32), pltpu.VMEM((tq, db), BF16)],
        compiler_params=_params("arbitrary", "arbitrary"),
        name="pool_layer",
    )(x, g, w_in, w_group, scale, w_out, final_g)


def _swa_kernel(sink_ref, q_ref, gate_ref, kp_ref, kc_ref, vp_ref, vc_ref, x_ref, wout_ref, o_ref, ybuf,
                *, lq, n_heads):
    l = pl.program_id(1)
    base, col = _band_mask(inclusive_far_edge=False)
    first = base & (col >= jnp.where(l == 0, BAND, 0))
    for sb in range(lq // BAND):
        rows = slice(sb * BAND, (sb + 1) * BAND)
        mask = first if sb == 0 else base
        for h in range(n_heads):
            cols = slice(h * HEAD_DIM, (h + 1) * HEAD_DIM)
            kcols = slice((h // SWA_GROUP) * HEAD_DIM, (h // SWA_GROUP + 1) * HEAD_DIM)
            if sb == 0:
                k = jnp.concatenate([kp_ref[:, kcols], kc_ref[0:BAND, kcols]], axis=0)
                v = jnp.concatenate([vp_ref[:, kcols], vc_ref[0:BAND, kcols]], axis=0)
            else:
                k = kc_ref[(sb - 1) * BAND:(sb + 1) * BAND, kcols]
                v = vc_ref[(sb - 1) * BAND:(sb + 1) * BAND, kcols]
            o, _ = _attend(q_ref[rows, cols], k, v, mask, sink_ref[h])
            ybuf[rows, cols] = (o * _silu(gate_ref[rows, cols].astype(F32))).astype(BF16)
    o_ref[...] = x_ref[...] + jnp.dot(ybuf[...], wout_ref[...], preferred_element_type=F32)


def _swa_layer(x, p, sinks, w_out, *, lq):
    b, s, d = x.shape
    db = w_out.shape[0]
    n_heads = db // HEAD_DIM
    kv = (n_heads // SWA_GROUP) * HEAD_DIM
    assert s % lq == 0 and lq % BAND == 0 and (2 * db) % kv == 0
    kcol = 2 * db // kv
    prev = lambda bi, l: jnp.maximum(l * (lq // BAND) - 1, 0)
    return pl.pallas_call(
        functools.partial(_swa_kernel, lq=lq, n_heads=n_heads),
        grid=(b, s // lq),
        in_specs=[
            pl.BlockSpec(memory_space=pltpu.SMEM),
            pl.BlockSpec((None, lq, db), lambda bi, l: (bi, l, 0)),
            pl.BlockSpec((None, lq, db), lambda bi, l: (bi, l, 1)),
            pl.BlockSpec((None, BAND, kv), lambda bi, l: (bi, prev(bi, l), kcol)),
            pl.BlockSpec((None, lq, kv), lambda bi, l: (bi, l, kcol)),
            pl.BlockSpec((None, BAND, kv), lambda bi, l: (bi, prev(bi, l), kcol + 1)),
            pl.BlockSpec((None, lq, kv), lambda bi, l: (bi, l, kcol + 1)),
            pl.BlockSpec((None, lq, d), lambda bi, l: (bi, l, 0)),
            pl.BlockSpec(w_out.shape, lambda bi, l: (0, 0)),
        ],
        out_specs=pl.BlockSpec((None, lq, d), lambda bi, l: (bi, l, 0)),
        out_shape=jax.ShapeDtypeStruct(x.shape, F32),
        scratch_shapes=[pltpu.VMEM((lq, db), BF16)],
        compiler_params=_params("arbitrary", "arbitrary"),
        name="swa_layer",
    )(sinks, p, p, p, p, p, p, x, w_out)


def _dil_attn_kernel(q_ref, kp_ref, kc_ref, vp_ref, vc_ref, o_ref, lse_ref, *, lq, hc):
    l = pl.program_id(2)
    base, col = _band_mask(inclusive_far_edge=True)
    first = base & (col >= jnp.where(l == 0, BAND, 0))
    for sb in range(lq // BAND):
        rows = slice(sb * BAND, (sb + 1) * BAND)
        mask = first if sb == 0 else base
        for h in range(hc):
            cols = slice(h * HEAD_DIM, (h + 1) * HEAD_DIM)
            if sb == 0:
                k = jnp.concatenate([kp_ref[:, cols], kc_ref[0:BAND, cols]], axis=0)
                v = jnp.concatenate([vp_ref[:, cols], vc_ref[0:BAND, cols]], axis=0)
            else:
                k = kc_ref[(sb - 1) * BAND:(sb + 1) * BAND, cols]
                v = vc_ref[(sb - 1) * BAND:(sb + 1) * BAND, cols]
            o, lse = _attend(q_ref[rows, cols], k, v, mask, None)
            o_ref[rows, cols] = o.astype(o_ref.dtype)
            lse_ref[rows, h:h + 1] = lse


def _dil_attn(p, *, group, dil, db, lq, cw):
    b, s, e = p.shape
    ls = s // dil
    hc = cw // HEAD_DIM
    assert s % dil == 0 and ls % lq == 0 and lq % BAND == 0 and db % cw == 0
    pv = p.reshape(b, ls, dil * e)
    epc, dpc = e // cw, db // cw
    qc, kc, vc = (3 * group) * dpc, (3 * group + 1) * dpc, (3 * group + 2) * dpc
    prev = lambda l: jnp.maximum(l * (lq // BAND) - 1, 0)
    cur = lambda off: pl.BlockSpec((None, lq, cw), lambda bi, r, l, c: (bi, l, r * epc + off + c))
    old = lambda off: pl.BlockSpec((None, BAND, cw), lambda bi, r, l, c: (bi, prev(l), r * epc + off + c))
    o, lse = pl.pallas_call(
        functools.partial(_dil_attn_kernel, lq=lq, hc=hc),
        grid=(b, dil, ls // lq, dpc),
        in_specs=[cur(qc), old(kc), cur(kc), old(vc), cur(vc)],
        out_specs=[
            pl.BlockSpec((None, lq, cw), lambda bi, r, l, c: (bi, l, r * dpc + c)),
            pl.BlockSpec((None, None, None, lq, hc), lambda bi, r, l, c: (bi, r, c, l, 0)),
        ],
        out_shape=[
            jax.ShapeDtypeStruct((b, ls, dil * db), BF16),
            jax.ShapeDtypeStruct((b, dil, dpc, ls, hc), F32),
        ],
        compiler_params=_params("arbitrary", "arbitrary", "arbitrary", "arbitrary"),
        name=f"dil_attn_g{group}",
    )(pv, pv, pv, pv, pv)
    return o.reshape(b, s, db), lse


def _merge_out_kernel(o0_ref, o1_ref, o2_ref, l0_ref, l1_ref, l2_ref, gate_ref, x_ref, exp_ref, wout_ref, o_ref):
    l0, l1, l2 = l0_ref[...], l1_ref[...], l2_ref[...]
    mx = jnp.maximum(jnp.maximum(l0, l1), l2)
    e0, e1, e2 = jnp.exp(l0 - mx), jnp.exp(l1 - mx), jnp.exp(l2 - mx)
    inv = 1.0 / (e0 + e1 + e2)
    y = None
    for e, o_g in ((e0, o0_ref), (e1, o1_ref), (e2, o2_ref)):
        w = jnp.dot((e * inv).astype(BF16), exp_ref[...], preferred_element_type=F32)
        t = w * o_g[...].astype(F32)
        y = t if y is None else y + t
    z = (y * _silu(gate_ref[...].astype(F32))).astype(BF16)
    o_ref[...] = x_ref[...] + jnp.dot(z, wout_ref[...], preferred_element_type=F32)


def _merge_out(x, os_, lses, p, w_out, *, tq):
    b, s, d = x.shape
    db = w_out.shape[0]
    n_heads = db // HEAD_DIM
    gate_col = p.shape[2] // db - 1
    expand = (jnp.arange(db)[None, :] // HEAD_DIM == jnp.arange(n_heads)[:, None]).astype(BF16)
    tok = lambda width, cb=0: pl.BlockSpec((None, tq, width), lambda bi, j: (bi, j, cb))
    return pl.pallas_call(
        _merge_out_kernel,
        grid=(b, s // tq),
        in_specs=[tok(db), tok(db), tok(db), tok(n_heads), tok(n_heads), tok(n_heads),
                  tok(db, gate_col), tok(d),
                  pl.BlockSpec(expand.shape, lambda bi, j: (0, 0)),
                  pl.BlockSpec(w_out.shape, lambda bi, j: (0, 0))],
        out_specs=tok(d),
        out_shape=jax.ShapeDtypeStruct(x.shape, F32),
        compiler_params=_params("arbitrary", "arbitrary"),
        name="merge_out",
    )(*os_, *lses, p, x, expand, w_out)


def _lse_to_token_order(lse):
    b, dil, nc, ls, hc = lse.shape
    return jnp.transpose(lse, (0, 3, 1, 2, 4)).reshape(b, ls * dil, nc * hc)


def _scale_cols(w, start, stop, factor):
    return w.at[:, start:stop].multiply(factor)


def kernel(x, norm_g, final_g, w_out, a_w_in, a_w_group, a_scale, b_w_in, b_sinks, c_w_in):
    b, s, d = x.shape
    depth = norm_g.shape[0]
    db = w_out.shape[1]
    qscale = HEAD_DIM ** -0.5
    tq = min(512, s)
    fg = final_g.reshape(1, d)
    for i in range(depth):
        kind, j = i % N_MIXERS, i // N_MIXERS
        g = norm_g[i].reshape(1, d)
        wo = w_out[i].astype(BF16)
        last = i == depth - 1
        if kind == 0:
            x = _pool_layer(x, g, a_w_in[j].astype(BF16), a_w_group[j].astype(BF16),
                            a_scale[j].reshape(1, db), wo, fg, final_norm=last, tq=tq)
        elif kind == 1:
            w = b_w_in[j]
            kvw = (w.shape[1] - 2 * db) // 2
            wq, wk, wv, wg = jnp.split(w, [db, db + kvw, db + 2 * kvw], axis=1)
            w = jnp.concatenate([wq * qscale, wg, wk, wv], axis=1).astype(BF16)
            p = _proj(x.reshape(b * s, d), g, w, tm=min(1024, b * s), tn=w.shape[1] // 2)
            x = _swa_layer(x, p.reshape(b, s, -1), b_sinks[j], wo, lq=tq)
        else:
            w = c_w_in[j]
            for gi in range(len(DILATIONS)):
                w = _scale_cols(w, 3 * gi * db, (3 * gi + 1) * db, qscale)
            p = _proj(x.reshape(b * s, d), g, w.astype(BF16), tm=min(1024, b * s), tn=db)
            p = p.reshape(b, s, -1)
            os_, lses = [], []
            for gi, dil in enumerate(DILATIONS):
                o, lse = _dil_attn(p, group=gi, dil=dil, db=db, lq=min(512, s // dil), cw=512)
                os_.append(o)
                lses.append(_lse_to_token_order(lse))
            x = _merge_out(x, os_, lses, p, wo, tq=tq)
        if last and kind != 0:
            raise NotImplementedError("final norm is fused into the pooling layer only")
    return x
```

```python
import functools

import jax
import jax.numpy as jnp
from jax import lax
from jax.experimental import pallas as pl
from jax.experimental.pallas import tpu as pltpu

HEAD_DIM = 64
PAIR = 2 * HEAD_DIM
RMS_EPS = 1e-5
BAND = 128
POOL_WINDOWS = (2, 4, 8, 16)
POOL_HALO = 16
SWA_GROUP = 8
DILATIONS = (1, 4, 16)
TILE = DILATIONS[-1] * BAND
N_MIXERS = 3
LANES = 128

V7X_VMEM_LIMIT_BYTES = 56 * 1024 * 1024
NEG = -1e30

F32 = jnp.float32
BF16 = jnp.bfloat16


def _params(*semantics):
    return pltpu.CompilerParams(dimension_semantics=semantics, vmem_limit_bytes=V7X_VMEM_LIMIT_BYTES)


def _rmsnorm(x, g):
    return x * lax.rsqrt(jnp.mean(x * x, axis=-1, keepdims=True) + RMS_EPS) * g


def _silu(g):
    return g / (1.0 + jnp.exp(-g))


def _band_mask(inclusive_far_edge):
    row = lax.broadcasted_iota(jnp.int32, (BAND, 2 * BAND), 0)
    col = lax.broadcasted_iota(jnp.int32, (BAND, 2 * BAND), 1)
    lo = row if inclusive_far_edge else row + 1
    return (col >= lo) & (col <= row + BAND), col


def _lane_halves():
    lane = lax.broadcasted_iota(jnp.int32, (1, PAIR), 1)
    first = (lane < HEAD_DIM).astype(F32)
    return first.astype(BF16), (1.0 - first).astype(BF16), lane < HEAD_DIM


def _attend_pairs(q_pairs, k_tile, v_tile, mask, sinks=None):
    n = len(q_pairs)
    za, zb, is_a = _lane_halves()
    lhs = jnp.concatenate([q * z for q in q_pairs for z in (za, zb)], axis=0)
    s = lax.dot_general(lhs, k_tile, (((1,), (1,)), ((), ())), preferred_element_type=F32)
    s = jnp.where(mask[None], s.reshape(2 * n, BAND, 2 * BAND), NEG)
    m = jnp.max(s, axis=-1, keepdims=True)
    if sinks is not None:
        m = jnp.concatenate([jnp.maximum(m[h], sinks[h])[None] for h in range(2 * n)], axis=0)
    e = jnp.exp(s - m).astype(BF16)
    ones = jnp.ones((2 * BAND, 1), BF16)
    v_aug = jnp.concatenate([jnp.concatenate([v_tile * za, ones * za], axis=1),
                             jnp.concatenate([v_tile * zb, ones * zb], axis=1)], axis=0)
    e_cat = jnp.concatenate([jnp.concatenate([e[2 * p], e[2 * p + 1]], axis=1) for p in range(n)], axis=0)
    acc = jnp.dot(e_cat, v_aug, preferred_element_type=F32)
    out = []
    for p in range(n):
        num = acc[p * BAND:(p + 1) * BAND, :PAIR]
        den = acc[p * BAND:(p + 1) * BAND, PAIR:]
        m_pair = jnp.where(is_a, m[2 * p], m[2 * p + 1])
        if sinks is not None:
            den = den + jnp.exp(jnp.where(is_a, sinks[2 * p], sinks[2 * p + 1]) - m_pair)
        out.append((num / den, m_pair, den))
    return out


def _proj_kernel(x_ref, g_ref, w_ref, o_ref, h_ref):
    @pl.when(pl.program_id(1) == 0)
    def _():
        h_ref[...] = _rmsnorm(x_ref[...], g_ref[...]).astype(BF16)

    o_ref[...] = jnp.dot(h_ref[...], w_ref[...], preferred_element_type=F32).astype(o_ref.dtype)


def _proj(x2d, g, w, *, tm, tn):
    t, d = x2d.shape
    e = w.shape[1]
    assert t % tm == 0 and e % tn == 0
    return pl.pallas_call(
        _proj_kernel,
        grid=(t // tm, e // tn),
        in_specs=[
            pl.BlockSpec((tm, d), lambda i, j: (i, 0)),
            pl.BlockSpec((1, d), lambda i, j: (0, 0)),
            pl.BlockSpec((d, tn), lambda i, j: (0, j)),
        ],
        out_specs=pl.BlockSpec((tm, tn), lambda i, j: (i, j)),
        out_shape=jax.ShapeDtypeStruct((t, e), BF16),
        scratch_shapes=[pltpu.VMEM((tm, d), BF16)],
        compiler_params=_params("arbitrary", "arbitrary"),
        name="in_proj",
    )(x2d, g, w)


def _matmul_kernel(a_ref, w_ref, o_ref):
    o_ref[...] = jnp.dot(a_ref[...], w_ref[...], preferred_element_type=F32).astype(o_ref.dtype)


def _matmul(a, w, *, tm, tn, name):
    t, d = a.shape
    e = w.shape[1]
    assert t % tm == 0 and e % tn == 0
    return pl.pallas_call(
        _matmul_kernel,
        grid=(t // tm, e // tn),
        in_specs=[pl.BlockSpec((tm, d), lambda i, j: (i, 0)), pl.BlockSpec((d, tn), lambda i, j: (0, j))],
        out_specs=pl.BlockSpec((tm, tn), lambda i, j: (i, j)),
        out_shape=jax.ShapeDtypeStruct((t, e), BF16),
        compiler_params=_params("arbitrary", "arbitrary"),
        name=name,
    )(a, w)


def _out_proj_kernel(x_ref, y_ref, w_ref, o_ref):
    o_ref[...] = x_ref[...] + jnp.dot(y_ref[...], w_ref[...], preferred_element_type=F32)


def _out_proj(x2d, y2d, w, *, tm):
    t, d = x2d.shape
    assert t % tm == 0
    return pl.pallas_call(
        _out_proj_kernel,
        grid=(t // tm,),
        in_specs=[pl.BlockSpec((tm, d), lambda i: (i, 0)),
                  pl.BlockSpec((tm, y2d.shape[1]), lambda i: (i, 0)),
                  pl.BlockSpec(w.shape, lambda i: (0, 0))],
        out_specs=pl.BlockSpec((tm, d), lambda i: (i, 0)),
        out_shape=jax.ShapeDtypeStruct(x2d.shape, F32),
        compiler_params=_params("arbitrary"),
        name="out_proj",
    )(x2d, y2d, w)


def _pool_layer_kernel(x_ref, g_ref, win_ref, wg_ref, sc_ref, wout_ref, fg_ref, o_ref, ubuf, zbuf,
                       *, tq, db, final_norm):
    j = pl.program_id(1)
    x = x_ref[...]
    h = _rmsnorm(x, g_ref[...]).astype(BF16)
    p = jnp.dot(h, win_ref[...], preferred_element_type=F32)

    @pl.when(j == 0)
    def _():
        ubuf[0:POOL_HALO, :] = jnp.zeros((POOL_HALO, db), F32)

    @pl.when(j > 0)
    def _():
        ubuf[0:POOL_HALO, :] = ubuf[tq:tq + POOL_HALO, :]

    ubuf[POOL_HALO:POOL_HALO + tq, :] = p[:, :db]

    t_abs = j * tq + lax.broadcasted_iota(jnp.int32, (tq, 1), 0)
    gd = db // len(POOL_WINDOWS)
    for gi, w in enumerate(POOL_WINDOWS):
        cols = slice(gi * gd, (gi + 1) * gd)
        u = ubuf[POOL_HALO:POOL_HALO + tq, cols]
        acc = u
        for k in range(1, w):
            acc = acc + ubuf[POOL_HALO - k:POOL_HALO - k + tq, cols]
        count = jnp.minimum(t_abs + 1, w).astype(F32)
        d = (acc / count - u).astype(BF16)
        y = jnp.dot(d, wg_ref[gi], preferred_element_type=F32) * sc_ref[:, cols]
        gate = p[:, db + gi * gd:db + (gi + 1) * gd]
        zbuf[:, cols] = (y * _silu(gate)).astype(BF16)

    out = x + jnp.dot(zbuf[...], wout_ref[...], preferred_element_type=F32)
    if final_norm:
        out = _rmsnorm(out, fg_ref[...])
    o_ref[...] = out


def _pool_layer(x, g, w_in, w_group, scale, w_out, final_g, *, final_norm, tq):
    b, s, d = x.shape
    db = w_out.shape[0]
    assert s % tq == 0 and tq >= POOL_HALO
    const2 = lambda bi, j: (0, 0)
    return pl.pallas_call(
        functools.partial(_pool_layer_kernel, tq=tq, db=db, final_norm=final_norm),
        grid=(b, s // tq),
        in_specs=[
            pl.BlockSpec((None, tq, d), lambda bi, j: (bi, j, 0)),
            pl.BlockSpec((1, d), const2),
            pl.BlockSpec(w_in.shape, const2),
            pl.BlockSpec(w_group.shape, lambda bi, j: (0, 0, 0)),
            pl.BlockSpec((1, db), const2),
            pl.BlockSpec(w_out.shape, const2),
            pl.BlockSpec((1, d), const2),
        ],
        out_specs=pl.BlockSpec((None, tq, d), lambda bi, j: (bi, j, 0)),
        out_shape=jax.ShapeDtypeStruct(x.shape, F32),
        scratch_shapes=[pltpu.VMEM((POOL_HALO + tq, db), F32), pltpu.VMEM((tq, db), BF16)],
        compiler_params=_params("arbitrary", "arbitrary"),
        name="pool_layer",
    )(x, g, w_in, w_group, scale, w_out, final_g)


def _swa_kernel(sink_ref, q_ref, gate_ref, kp_ref, kc_ref, vp_ref, vc_ref, x_ref, wout_ref, o_ref,
                ybuf, kdup, vdup, *, lq, n_heads):
    l = pl.program_id(1)
    base, col = _band_mask(inclusive_far_edge=False)
    first = base & (col >= jnp.where(l == 0, BAND, 0))
    n_kv = n_heads // SWA_GROUP
    pairs_per_kv = SWA_GROUP // 2
    for j in range(n_kv):
        kcols = slice(j * HEAD_DIM, (j + 1) * HEAD_DIM)
        for src_p, src_c, dst in ((kp_ref, kc_ref, kdup), (vp_ref, vc_ref, vdup)):
            dst[j, 0:BAND, :] = jnp.concatenate([src_p[:, kcols]] * 2, axis=1)
            dst[j, BAND:BAND + lq, :] = jnp.concatenate([src_c[:, kcols]] * 2, axis=1)
    for sb in range(lq // BAND):
        rows = slice(sb * BAND, (sb + 1) * BAND)
        mask = first if sb == 0 else base
        for j in range(n_kv):
            pair_cols = [slice((j * pairs_per_kv + p) * PAIR, (j * pairs_per_kv + p + 1) * PAIR)
                         for p in range(pairs_per_kv)]
            res = _attend_pairs([q_ref[rows, c] for c in pair_cols],
                                kdup[j, sb * BAND:(sb + 2) * BAND, :], vdup[j, sb * BAND:(sb + 2) * BAND, :],
                                mask, [sink_ref[j * SWA_GROUP + h] for h in range(SWA_GROUP)])
            for c, (o, _, _) in zip(pair_cols, res):
                ybuf[rows, c] = (o * _silu(gate_ref[rows, c].astype(F32))).astype(BF16)
    o_ref[...] = x_ref[...] + jnp.dot(ybuf[...], wout_ref[...], preferred_element_type=F32)


def _swa_layer(x, p, sinks, w_out, *, lq):
    b, s, d = x.shape
    db = w_out.shape[0]
    n_heads = db // HEAD_DIM
    n_kv = n_heads // SWA_GROUP
    kv = n_kv * HEAD_DIM
    assert s % lq == 0 and lq % BAND == 0 and (2 * db) % kv == 0
    kcol = 2 * db // kv
    prev = lambda bi, l: jnp.maximum(l * (lq // BAND) - 1, 0)
    return pl.pallas_call(
        functools.partial(_swa_kernel, lq=lq, n_heads=n_heads),
        grid=(b, s // lq),
        in_specs=[
            pl.BlockSpec(memory_space=pltpu.SMEM),
            pl.BlockSpec((None, lq, db), lambda bi, l: (bi, l, 0)),
            pl.BlockSpec((None, lq, db), lambda bi, l: (bi, l, 1)),
            pl.BlockSpec((None, BAND, kv), lambda bi, l: (bi, prev(bi, l), kcol)),
            pl.BlockSpec((None, lq, kv), lambda bi, l: (bi, l, kcol)),
            pl.BlockSpec((None, BAND, kv), lambda bi, l: (bi, prev(bi, l), kcol + 1)),
            pl.BlockSpec((None, lq, kv), lambda bi, l: (bi, l, kcol + 1)),
            pl.BlockSpec((None, lq, d), lambda bi, l: (bi, l, 0)),
            pl.BlockSpec(w_out.shape, lambda bi, l: (0, 0)),
        ],
        out_specs=pl.BlockSpec((None, lq, d), lambda bi, l: (bi, l, 0)),
        out_shape=jax.ShapeDtypeStruct(x.shape, F32),
        scratch_shapes=[pltpu.VMEM((lq, db), BF16),
                        pltpu.VMEM((n_kv, BAND + lq, PAIR), BF16),
                        pltpu.VMEM((n_kv, BAND + lq, PAIR), BF16)],
        compiler_params=_params("arbitrary", "arbitrary"),
        name="swa_layer",
    )(sinks, p, p, p, p, p, p, x, w_out)


def _norm_streams_kernel(x_ref, g_ref, h0_ref, h1_ref, h2_ref, slab):
    d = x_ref.shape[1]
    ssq = jnp.zeros((TILE, 1), F32)
    for c in range(d // LANES):
        xc = x_ref[:, c * LANES:(c + 1) * LANES]
        ssq = ssq + jnp.sum(xc * xc, axis=-1, keepdims=True)
    inv = lax.rsqrt(ssq / d + RMS_EPS)
    for c in range(d // LANES):
        cols = slice(c * LANES, (c + 1) * LANES)
        hc = x_ref[:, cols] * inv * g_ref[:, cols]
        h0_ref[:, cols] = hc.astype(BF16)
        slab[c % 2] = hc
        for dil, dst in ((DILATIONS[1], h1_ref), (DILATIONS[2], h2_ref)):
            n = TILE // dil
            for r in range(dil):
                dst[r * n:(r + 1) * n, cols] = slab[c % 2, pl.ds(r, n, stride=dil), :].astype(BF16)


def _norm_streams(x, g):
    b, s, d = x.shape
    assert s % TILE == 0 and d % LANES == 0
    spec = pl.BlockSpec((None, TILE, d), lambda bi, i: (bi, i, 0))
    out = jax.ShapeDtypeStruct(x.shape, BF16)
    return pl.pallas_call(
        _norm_streams_kernel,
        grid=(b, s // TILE),
        in_specs=[spec, pl.BlockSpec((1, d), lambda bi, i: (0, 0))],
        out_specs=[spec, spec, spec],
        out_shape=[out, out, out],
        scratch_shapes=[pltpu.VMEM((2, TILE, LANES), F32)],
        compiler_params=_params("arbitrary", "arbitrary"),
        name="norm_streams",
    )(x, g)


def _dilated_kernel(q0, k0, v0, gate, q1, k1, v1, q2, k2, v2, y_ref,
                    ck0, cv0, ck1, cv1, ck2, cv2, onat, lnat, *, cw):
    i = pl.program_id(2)
    groups = ((q0, k0, v0, ck0, cv0), (q1, k1, v1, ck1, cv1), (q2, k2, v2, ck2, cv2))
    base, col = _band_mask(inclusive_far_edge=True)
    first = base & (col >= jnp.where(i == 0, BAND, 0))

    @pl.when(i == 0)
    def _():
        for _, _, _, ck, cv in groups:
            ck[...] = jnp.zeros(ck.shape, BF16)
            cv[...] = jnp.zeros(cv.shape, BF16)

    for g, (dil, (q_ref, k_ref, v_ref, ck, cv)) in enumerate(zip(DILATIONS, groups)):
        n_sb = TILE // (dil * BAND)
        for r in range(dil):
            for sb in range(n_sb):
                row0 = (r * n_sb + sb) * BAND
                for pr in range(cw // PAIR):
                    lanes = slice(pr * PAIR, (pr + 1) * PAIR)
                    if sb == 0:
                        k_t = jnp.concatenate([ck[r * BAND:(r + 1) * BAND, lanes], k_ref[row0:row0 + BAND, lanes]], axis=0)
                        v_t = jnp.concatenate([cv[r * BAND:(r + 1) * BAND, lanes], v_ref[row0:row0 + BAND, lanes]], axis=0)
                    else:
                        k_t = k_ref[row0 - BAND:row0 + BAND, lanes]
                        v_t = v_ref[row0 - BAND:row0 + BAND, lanes]
                    (o, m, den), = _attend_pairs([q_ref[row0:row0 + BAND, lanes]], k_t, v_t,
                                                 first if sb == 0 else base)
                    dst = pl.ds(sb * BAND * dil + r, BAND, stride=dil) if dil > 1 else pl.ds(sb * BAND, BAND)
                    onat[g, pr, dst, :] = o
                    lnat[g, pr, dst, :] = m + jnp.log(den)
        for r in range(dil):
            last = ((r + 1) * n_sb - 1) * BAND
            ck[r * BAND:(r + 1) * BAND, :] = k_ref[last:last + BAND, :]
            cv[r * BAND:(r + 1) * BAND, :] = v_ref[last:last + BAND, :]

    for pr in range(cw // PAIR):
        lanes = slice(pr * PAIR, (pr + 1) * PAIR)
        l0, l1, l2 = lnat[0, pr], lnat[1, pr], lnat[2, pr]
        mx = jnp.maximum(jnp.maximum(l0, l1), l2)
        e0, e1, e2 = jnp.exp(l0 - mx), jnp.exp(l1 - mx), jnp.exp(l2 - mx)
        y = (e0 * onat[0, pr] + e1 * onat[1, pr] + e2 * onat[2, pr]) / (e0 + e1 + e2)
        y_ref[:, lanes] = (y * _silu(gate[:, lanes].astype(F32))).astype(BF16)


def _dilated_attention(p0, p1, p2, *, db, cw):
    b, s, _ = p0.shape
    assert s % TILE == 0 and db % cw == 0 and cw % PAIR == 0
    nc = db // cw
    blk = lambda part: pl.BlockSpec((None, TILE, cw), lambda bi, c, i: (bi, i, part * nc + c))
    carry = lambda dil: pltpu.VMEM((dil * BAND, cw), BF16)
    nat = pltpu.VMEM((len(DILATIONS), cw // PAIR, TILE, LANES), F32)
    return pl.pallas_call(
        functools.partial(_dilated_kernel, cw=cw),
        grid=(b, nc, s // TILE),
        in_specs=[blk(0), blk(1), blk(2), blk(3), blk(0), blk(1), blk(2), blk(0), blk(1), blk(2)],
        out_specs=pl.BlockSpec((None, TILE, cw), lambda bi, c, i: (bi, i, c)),
        out_shape=jax.ShapeDtypeStruct((b, s, db), BF16),
        scratch_shapes=[carry(DILATIONS[0]), carry(DILATIONS[0]), carry(DILATIONS[1]), carry(DILATIONS[1]),
                        carry(DILATIONS[2]), carry(DILATIONS[2]), nat, nat],
        compiler_params=_params("arbitrary", "arbitrary", "arbitrary"),
        name="dilated_attention",
    )(p0, p0, p0, p0, p1, p1, p1, p2, p2, p2)


def kernel(x, norm_g, final_g, w_out, a_w_in, a_w_group, a_scale, b_w_in, b_sinks, c_w_in):
    b, s, d = x.shape
    depth = norm_g.shape[0]
    db = w_out.shape[1]
    qscale = HEAD_DIM ** -0.5
    tq = min(512, s)
    tm = min(1024, b * s)
    fg = final_g.reshape(1, d)
    for i in range(depth):
        kind, j = i % N_MIXERS, i // N_MIXERS
        g = norm_g[i].reshape(1, d)
        wo = w_out[i].astype(BF16)
        last = i == depth - 1
        if kind == 0:
            x = _pool_layer(x, g, a_w_in[j].astype(BF16), a_w_group[j].astype(BF16),
                            a_scale[j].reshape(1, db), wo, fg, final_norm=last, tq=tq)
        elif kind == 1:
            w = b_w_in[j]
            kvw = (w.shape[1] - 2 * db) // 2
            wq, wk, wv, wg = jnp.split(w, [db, db + kvw, db + 2 * kvw], axis=1)
            w = jnp.concatenate([wq * qscale, wg, wk, wv], axis=1).astype(BF16)
            p = _proj(x.reshape(b * s, d), g, w, tm=tm, tn=w.shape[1] // 2)
            x = _swa_layer(x, p.reshape(b, s, -1), b_sinks[j], wo, lq=tq)
        else:
            w = c_w_in[j].reshape(d, 3 * len(DILATIONS) + 1, db)
            part = lambda gi: [w[:, 3 * gi] * qscale, w[:, 3 * gi + 1], w[:, 3 * gi + 2]]
            ws = [jnp.concatenate(part(0) + [w[:, -1]], axis=1), jnp.concatenate(part(1), axis=1),
                  jnp.concatenate(part(2), axis=1)]
            hs = _norm_streams(x, g)
            ps = [_matmul(h.reshape(b * s, d), wg_.astype(BF16), tm=tm, tn=db, name=f"dil_proj{gi}").reshape(b, s, -1)
                  for gi, (h, wg_) in enumerate(zip(hs, ws))]
            y = _dilated_attention(*ps, db=db, cw=2 * PAIR)
            x = _out_proj(x.reshape(b * s, d), y.reshape(b * s, db), wo, tm=tm).reshape(b, s, d)
        if last and kind != 0:
            raise NotImplementedError("final norm is fused into the pooling layer only")
    return x
```

```python
import functools

import jax
import jax.numpy as jnp
from jax import lax
from jax.experimental import pallas as pl
from jax.experimental.pallas import tpu as pltpu

HEAD_DIM = 64
PAIR = 2 * HEAD_DIM
RMS_EPS = 1e-5
BAND = 128
POOL_WINDOWS = (2, 4, 8, 16)
POOL_HALO = 16
SWA_GROUP = 8
DILATIONS = (1, 4, 16)
TILE = DILATIONS[-1] * BAND
N_MIXERS = 3
LANES = 128

V7X_VMEM_LIMIT_BYTES = 56 * 1024 * 1024
NEG = -1e30
LOG2E = 1.4426950408889634

F32 = jnp.float32
BF16 = jnp.bfloat16


def _params(*semantics):
    return pltpu.CompilerParams(dimension_semantics=semantics, vmem_limit_bytes=V7X_VMEM_LIMIT_BYTES)


def _rmsnorm(x, g):
    return x * lax.rsqrt(jnp.mean(x * x, axis=-1, keepdims=True) + RMS_EPS) * g


def _silu(g):
    return g / (1.0 + jnp.exp(-g))


def _band_bias(inclusive_far_edge, at_sequence_start):
    col = lax.broadcasted_iota(jnp.int32, (2 * BAND, BAND), 0)
    row = lax.broadcasted_iota(jnp.int32, (2 * BAND, BAND), 1)
    lo = row if inclusive_far_edge else row + 1
    valid = (col >= lo) & (col <= row + BAND)
    first = valid & (col >= jnp.where(at_sequence_start, BAND, 0))
    to_bias = lambda ok: jnp.where(ok, 0.0, NEG).astype(BF16)
    return to_bias(valid), to_bias(first)


def _row_selector():
    row = lax.broadcasted_iota(jnp.int32, (BAND, BAND), 0)
    col = lax.broadcasted_iota(jnp.int32, (BAND, BAND), 1)
    return (row == col).astype(F32).astype(BF16)


def _lane_halves():
    lane = lax.broadcasted_iota(jnp.int32, (1, PAIR), 1)
    first = (lane < HEAD_DIM).astype(F32)
    return first.astype(BF16), (1.0 - first).astype(BF16), lane < HEAD_DIM


def _attend_pairs(q_pairs, k_tile, v_tile, bias, sinks=None):
    n = len(q_pairs)
    za, zb, is_a = _lane_halves()
    sel = _row_selector()
    lhs = jnp.concatenate([jnp.concatenate([q * z, sel], axis=1) for q in q_pairs for z in (za, zb)], axis=0)
    rhs = jnp.concatenate([k_tile, bias], axis=1)
    s = lax.dot_general(lhs, rhs, (((1,), (1,)), ((), ())), preferred_element_type=F32)
    s = s.reshape(2 * n, BAND, 2 * BAND)
    m = jnp.max(s, axis=-1, keepdims=True)
    if sinks is not None:
        m = jnp.concatenate([jnp.maximum(m[h], sinks[h])[None] for h in range(2 * n)], axis=0)
    e = jnp.exp2(s - m).astype(BF16)
    ones = jnp.ones((2 * BAND, 1), BF16)
    v_aug = jnp.concatenate([jnp.concatenate([v_tile * za, ones * za], axis=1),
                             jnp.concatenate([v_tile * zb, ones * zb], axis=1)], axis=0)
    e_cat = jnp.concatenate([jnp.concatenate([e[2 * p], e[2 * p + 1]], axis=1) for p in range(n)], axis=0)
    acc = jnp.dot(e_cat, v_aug, preferred_element_type=F32)
    out = []
    for p in range(n):
        num = acc[p * BAND:(p + 1) * BAND, :PAIR]
        den = acc[p * BAND:(p + 1) * BAND, PAIR:]
        m_pair = jnp.where(is_a, m[2 * p], m[2 * p + 1])
        if sinks is not None:
            den = den + jnp.exp2(jnp.where(is_a, sinks[2 * p], sinks[2 * p + 1]) - m_pair)
        out.append((num, m_pair, den))
    return out


def _proj_kernel(x_ref, g_ref, w_ref, o_ref, h_ref):
    @pl.when(pl.program_id(1) == 0)
    def _():
        h_ref[...] = _rmsnorm(x_ref[...], g_ref[...]).astype(BF16)

    o_ref[...] = jnp.dot(h_ref[...], w_ref[...], preferred_element_type=F32).astype(o_ref.dtype)


def _proj(x2d, g, w, *, tm, tn):
    t, d = x2d.shape
    e = w.shape[1]
    assert t % tm == 0 and e % tn == 0
    return pl.pallas_call(
        _proj_kernel,
        grid=(t // tm, e // tn),
        in_specs=[
            pl.BlockSpec((tm, d), lambda i, j: (i, 0)),
            pl.BlockSpec((1, d), lambda i, j: (0, 0)),
            pl.BlockSpec((d, tn), lambda i, j: (0, j)),
        ],
        out_specs=pl.BlockSpec((tm, tn), lambda i, j: (i, j)),
        out_shape=jax.ShapeDtypeStruct((t, e), BF16),
        scratch_shapes=[pltpu.VMEM((tm, d), BF16)],
        compiler_params=_params("arbitrary", "arbitrary"),
        name="in_proj",
    )(x2d, g, w)


def _matmul_kernel(a_ref, w_ref, o_ref):
    o_ref[...] = jnp.dot(a_ref[...], w_ref[...], preferred_element_type=F32).astype(o_ref.dtype)


def _matmul(a, w, *, tm, tn, name):
    t, d = a.shape
    e = w.shape[1]
    assert t % tm == 0 and e % tn == 0
    return pl.pallas_call(
        _matmul_kernel,
        grid=(t // tm, e // tn),
        in_specs=[pl.BlockSpec((tm, d), lambda i, j: (i, 0)), pl.BlockSpec((d, tn), lambda i, j: (0, j))],
        out_specs=pl.BlockSpec((tm, tn), lambda i, j: (i, j)),
        out_shape=jax.ShapeDtypeStruct((t, e), BF16),
        compiler_params=_params("arbitrary", "arbitrary"),
        name=name,
    )(a, w)


def _out_proj_kernel(x_ref, y_ref, w_ref, o_ref):
    o_ref[...] = x_ref[...] + jnp.dot(y_ref[...], w_ref[...], preferred_element_type=F32)


def _out_proj(x2d, y2d, w, *, tm):
    t, d = x2d.shape
    assert t % tm == 0
    return pl.pallas_call(
        _out_proj_kernel,
        grid=(t // tm,),
        in_specs=[pl.BlockSpec((tm, d), lambda i: (i, 0)),
                  pl.BlockSpec((tm, y2d.shape[1]), lambda i: (i, 0)),
                  pl.BlockSpec(w.shape, lambda i: (0, 0))],
        out_specs=pl.BlockSpec((tm, d), lambda i: (i, 0)),
        out_shape=jax.ShapeDtypeStruct(x2d.shape, F32),
        compiler_params=_params("arbitrary"),
        name="out_proj",
    )(x2d, y2d, w)


def _pool_layer_kernel(x_ref, g_ref, win_ref, wg_ref, sc_ref, wout_ref, fg_ref, o_ref, ubuf, zbuf,
                       *, tq, db, final_norm):
    j = pl.program_id(1)
    x = x_ref[...]
    h = _rmsnorm(x, g_ref[...]).astype(BF16)
    p = jnp.dot(h, win_ref[...], preferred_element_type=F32)

    @pl.when(j == 0)
    def _():
        ubuf[0:POOL_HALO, :] = jnp.zeros((POOL_HALO, db), F32)

    @pl.when(j > 0)
    def _():
        ubuf[0:POOL_HALO, :] = ubuf[tq:tq + POOL_HALO, :]

    ubuf[POOL_HALO:POOL_HALO + tq, :] = p[:, :db]

    t_abs = j * tq + lax.broadcasted_iota(jnp.int32, (tq, 1), 0)
    gd = db // len(POOL_WINDOWS)
    for gi, w in enumerate(POOL_WINDOWS):
        cols = slice(gi * gd, (gi + 1) * gd)
        ext = ubuf[:, cols]
        acc, span = ext, 1
        while span < w:
            acc = acc + pltpu.roll(acc, span, 0)
            span *= 2
        u, acc = ext[POOL_HALO:], acc[POOL_HALO:]
        count = jnp.minimum(t_abs + 1, w).astype(F32)
        d = (acc / count - u).astype(BF16)
        y = jnp.dot(d, wg_ref[gi], preferred_element_type=F32) * sc_ref[:, cols]
        gate = p[:, db + gi * gd:db + (gi + 1) * gd]
        zbuf[:, cols] = (y * _silu(gate)).astype(BF16)

    out = x + jnp.dot(zbuf[...], wout_ref[...], preferred_element_type=F32)
    if final_norm:
        out = _rmsnorm(out, fg_ref[...])
    o_ref[...] = out


def _pool_layer(x, g, w_in, w_group, scale, w_out, final_g, *, final_norm, tq):
    b, s, d = x.shape
    db = w_out.shape[0]
    assert s % tq == 0 and tq >= POOL_HALO
    const2 = lambda bi, j: (0, 0)
    return pl.pallas_call(
        functools.partial(_pool_layer_kernel, tq=tq, db=db, final_norm=final_norm),
        grid=(b, s // tq),
        in_specs=[
            pl.BlockSpec((None, tq, d), lambda bi, j: (bi, j, 0)),
            pl.BlockSpec((1, d), const2),
            pl.BlockSpec(w_in.shape, const2),
            pl.BlockSpec(w_group.shape, lambda bi, j: (0, 0, 0)),
            pl.BlockSpec((1, db), const2),
            pl.BlockSpec(w_out.shape, const2),
            pl.BlockSpec((1, d), const2),
        ],
        out_specs=pl.BlockSpec((None, tq, d), lambda bi, j: (bi, j, 0)),
        out_shape=jax.ShapeDtypeStruct(x.shape, F32),
        scratch_shapes=[pltpu.VMEM((POOL_HALO + tq, db), F32), pltpu.VMEM((tq, db), BF16)],
        compiler_params=_params("arbitrary", "arbitrary"),
        name="pool_layer",
    )(x, g, w_in, w_group, scale, w_out, final_g)


def _swa_kernel(sink_ref, q_ref, gate_ref, kp_ref, kc_ref, vp_ref, vc_ref, x_ref, wout_ref, o_ref,
                ybuf, kdup, vdup, *, lq, n_heads):
    l = pl.program_id(1)
    base, first = _band_bias(inclusive_far_edge=False, at_sequence_start=l == 0)
    n_kv = n_heads // SWA_GROUP
    pairs_per_kv = SWA_GROUP // 2
    for j in range(n_kv):
        kcols = slice(j * HEAD_DIM, (j + 1) * HEAD_DIM)
        for src_p, src_c, dst in ((kp_ref, kc_ref, kdup), (vp_ref, vc_ref, vdup)):
            dst[j, 0:BAND, :] = jnp.concatenate([src_p[:, kcols]] * 2, axis=1)
            dst[j, BAND:BAND + lq, :] = jnp.concatenate([src_c[:, kcols]] * 2, axis=1)
    for sb in range(lq // BAND):
        rows = slice(sb * BAND, (sb + 1) * BAND)
        bias = first if sb == 0 else base
        for j in range(n_kv):
            pair_cols = [slice((j * pairs_per_kv + p) * PAIR, (j * pairs_per_kv + p + 1) * PAIR)
                         for p in range(pairs_per_kv)]
            for p, c in enumerate(pair_cols):
                (num, _, den), = _attend_pairs(
                    [q_ref[rows, c]], kdup[j, sb * BAND:(sb + 2) * BAND, :], vdup[j, sb * BAND:(sb + 2) * BAND, :],
                    bias, [sink_ref[j * SWA_GROUP + 2 * p + h] * LOG2E for h in range(2)])
                ybuf[rows, c] = (num / den * _silu(gate_ref[rows, c].astype(F32))).astype(BF16)
    o_ref[...] = x_ref[...] + jnp.dot(ybuf[...], wout_ref[...], preferred_element_type=F32)


def _swa_layer(x, p, sinks, w_out, *, lq):
    b, s, d = x.shape
    db = w_out.shape[0]
    n_heads = db // HEAD_DIM
    n_kv = n_heads // SWA_GROUP
    kv = n_kv * HEAD_DIM
    assert s % lq == 0 and lq % BAND == 0 and (2 * db) % kv == 0
    kcol = 2 * db // kv
    prev = lambda bi, l: jnp.maximum(l * (lq // BAND) - 1, 0)
    return pl.pallas_call(
        functools.partial(_swa_kernel, lq=lq, n_heads=n_heads),
        grid=(b, s // lq),
        in_specs=[
            pl.BlockSpec(memory_space=pltpu.SMEM),
            pl.BlockSpec((None, lq, db), lambda bi, l: (bi, l, 0)),
            pl.BlockSpec((None, lq, db), lambda bi, l: (bi, l, 1)),
            pl.BlockSpec((None, BAND, kv), lambda bi, l: (bi, prev(bi, l), kcol)),
            pl.BlockSpec((None, lq, kv), lambda bi, l: (bi, l, kcol)),
            pl.BlockSpec((None, BAND, kv), lambda bi, l: (bi, prev(bi, l), kcol + 1)),
            pl.BlockSpec((None, lq, kv), lambda bi, l: (bi, l, kcol + 1)),
            pl.BlockSpec((None, lq, d), lambda bi, l: (bi, l, 0)),
            pl.BlockSpec(w_out.shape, lambda bi, l: (0, 0)),
        ],
        out_specs=pl.BlockSpec((None, lq, d), lambda bi, l: (bi, l, 0)),
        out_shape=jax.ShapeDtypeStruct(x.shape, F32),
        scratch_shapes=[pltpu.VMEM((lq, db), BF16),
                        pltpu.VMEM((n_kv, BAND + lq, PAIR), BF16),
                        pltpu.VMEM((n_kv, BAND + lq, PAIR), BF16)],
        compiler_params=_params("arbitrary", "arbitrary"),
        name="swa_layer",
    )(sinks, p, p, p, p, p, p, x, w_out)


def _norm_streams_kernel(x_ref, g_ref, h0_ref, h1_ref, h2_ref, slab):
    d = x_ref.shape[1]
    ssq = jnp.zeros((TILE, 1), F32)
    for c in range(d // LANES):
        xc = x_ref[:, c * LANES:(c + 1) * LANES]
        ssq = ssq + jnp.sum(xc * xc, axis=-1, keepdims=True)
    inv = lax.rsqrt(ssq / d + RMS_EPS)
    for c in range(d // LANES):
        cols = slice(c * LANES, (c + 1) * LANES)
        hc = x_ref[:, cols] * inv * g_ref[:, cols]
        h0_ref[:, cols] = hc.astype(BF16)
        slab[c % 2] = hc
        for dil, dst in ((DILATIONS[1], h1_ref), (DILATIONS[2], h2_ref)):
            n = TILE // dil
            for r in range(dil):
                dst[r * n:(r + 1) * n, cols] = slab[c % 2, pl.ds(r, n, stride=dil), :].astype(BF16)


def _norm_streams(x, g):
    b, s, d = x.shape
    assert s % TILE == 0 and d % LANES == 0
    spec = pl.BlockSpec((None, TILE, d), lambda bi, i: (bi, i, 0))
    out = jax.ShapeDtypeStruct(x.shape, BF16)
    return pl.pallas_call(
        _norm_streams_kernel,
        grid=(b, s // TILE),
        in_specs=[spec, pl.BlockSpec((1, d), lambda bi, i: (0, 0))],
        out_specs=[spec, spec, spec],
        out_shape=[out, out, out],
        scratch_shapes=[pltpu.VMEM((2, TILE, LANES), F32)],
        compiler_params=_params("arbitrary", "arbitrary"),
        name="norm_streams",
    )(x, g)


def _dilated_kernel(q0, k0, v0, gate, q1, k1, v1, q2, k2, v2, y_ref,
                    ck0, cv0, ck1, cv1, ck2, cv2, out_nat, lse_nat, *, cw):
    i = pl.program_id(2)
    groups = ((q0, k0, v0, ck0, cv0), (q1, k1, v1, ck1, cv1), (q2, k2, v2, ck2, cv2))
    base, first = _band_bias(inclusive_far_edge=True, at_sequence_start=i == 0)

    @pl.when(i == 0)
    def _():
        for _, _, _, ck, cv in groups:
            ck[...] = jnp.zeros(ck.shape, BF16)
            cv[...] = jnp.zeros(cv.shape, BF16)

    for g, (dil, (q_ref, k_ref, v_ref, ck, cv)) in enumerate(zip(DILATIONS, groups)):
        n_sb = TILE // (dil * BAND)
        for r in range(dil):
            for sb in range(n_sb):
                row0 = (r * n_sb + sb) * BAND
                for pr in range(cw // PAIR):
                    lanes = slice(pr * PAIR, (pr + 1) * PAIR)
                    if sb == 0:
                        k_t = jnp.concatenate([ck[r * BAND:(r + 1) * BAND, lanes], k_ref[row0:row0 + BAND, lanes]], axis=0)
                        v_t = jnp.concatenate([cv[r * BAND:(r + 1) * BAND, lanes], v_ref[row0:row0 + BAND, lanes]], axis=0)
                    else:
                        k_t = k_ref[row0 - BAND:row0 + BAND, lanes]
                        v_t = v_ref[row0 - BAND:row0 + BAND, lanes]
                    (num, m, den), = _attend_pairs([q_ref[row0:row0 + BAND, lanes]], k_t, v_t,
                                                   first if sb == 0 else base)
                    dst = pl.ds(sb * BAND * dil + r, BAND, stride=dil) if dil > 1 else pl.ds(sb * BAND, BAND)
                    out_nat[g, pr, dst, :] = num / den
                    lse_nat[g, pr, dst, :] = m + jnp.log2(den)
        for r in range(dil):
            last = ((r + 1) * n_sb - 1) * BAND
            ck[r * BAND:(r + 1) * BAND, :] = k_ref[last:last + BAND, :]
            cv[r * BAND:(r + 1) * BAND, :] = v_ref[last:last + BAND, :]

    for pr in range(cw // PAIR):
        lanes = slice(pr * PAIR, (pr + 1) * PAIR)
        l0, l1, l2 = lse_nat[0, pr], lse_nat[1, pr], lse_nat[2, pr]
        mx = jnp.maximum(jnp.maximum(l0, l1), l2)
        e0, e1, e2 = jnp.exp2(l0 - mx), jnp.exp2(l1 - mx), jnp.exp2(l2 - mx)
        y = (e0 * out_nat[0, pr] + e1 * out_nat[1, pr] + e2 * out_nat[2, pr]) / (e0 + e1 + e2)
        y_ref[:, lanes] = (y * _silu(gate[:, lanes].astype(F32))).astype(BF16)


def _dilated_attention(p0, p1, p2, *, db, cw):
    b, s, _ = p0.shape
    assert s % TILE == 0 and db % cw == 0 and cw % PAIR == 0
    nc = db // cw
    blk = lambda part: pl.BlockSpec((None, TILE, cw), lambda bi, c, i: (bi, i, part * nc + c))
    carry = lambda dil: pltpu.VMEM((dil * BAND, cw), BF16)
    nat = pltpu.VMEM((len(DILATIONS), cw // PAIR, TILE, LANES), F32)
    return pl.pallas_call(
        functools.partial(_dilated_kernel, cw=cw),
        grid=(b, nc, s // TILE),
        in_specs=[blk(0), blk(1), blk(2), blk(3), blk(0), blk(1), blk(2), blk(0), blk(1), blk(2)],
        out_specs=pl.BlockSpec((None, TILE, cw), lambda bi, c, i: (bi, i, c)),
        out_shape=jax.ShapeDtypeStruct((b, s, db), BF16),
        scratch_shapes=[carry(DILATIONS[0]), carry(DILATIONS[0]), carry(DILATIONS[1]), carry(DILATIONS[1]),
                        carry(DILATIONS[2]), carry(DILATIONS[2]), nat, nat],
        compiler_params=_params("arbitrary", "arbitrary", "arbitrary"),
        name="dilated_attention",
    )(p0, p0, p0, p0, p1, p1, p1, p2, p2, p2)


def kernel(x, norm_g, final_g, w_out, a_w_in, a_w_group, a_scale, b_w_in, b_sinks, c_w_in):
    b, s, d = x.shape
    depth = norm_g.shape[0]
    db = w_out.shape[1]
    qscale = HEAD_DIM ** -0.5 * LOG2E
    tq = min(512, s)
    tm = min(1024, b * s)
    fg = final_g.reshape(1, d)
    for i in range(depth):
        kind, j = i % N_MIXERS, i // N_MIXERS
        g = norm_g[i].reshape(1, d)
        wo = w_out[i].astype(BF16)
        last = i == depth - 1
        if kind == 0:
            x = _pool_layer(x, g, a_w_in[j].astype(BF16), a_w_group[j].astype(BF16),
                            a_scale[j].reshape(1, db), wo, fg, final_norm=last, tq=tq)
        elif kind == 1:
            w = b_w_in[j]
            kvw = (w.shape[1] - 2 * db) // 2
            wq, wk, wv, wg = jnp.split(w, [db, db + kvw, db + 2 * kvw], axis=1)
            w = jnp.concatenate([wq * qscale, wg, wk, wv], axis=1).astype(BF16)
            p = _proj(x.reshape(b * s, d), g, w, tm=tm, tn=w.shape[1] // 2)
            x = _swa_layer(x, p.reshape(b, s, -1), b_sinks[j], wo, lq=tq)
        else:
            w = c_w_in[j].reshape(d, 3 * len(DILATIONS) + 1, db)
            part = lambda gi: [w[:, 3 * gi] * qscale, w[:, 3 * gi + 1], w[:, 3 * gi + 2]]
            ws = [jnp.concatenate(part(0) + [w[:, -1]], axis=1), jnp.concatenate(part(1), axis=1),
                  jnp.concatenate(part(2), axis=1)]
            hs = _norm_streams(x, g)
            ps = [_matmul(h.reshape(b * s, d), wg_.astype(BF16), tm=tm, tn=db, name=f"dil_proj{gi}").reshape(b, s, -1)
                  for gi, (h, wg_) in enumerate(zip(hs, ws))]
            y = _dilated_attention(*ps, db=db, cw=2 * PAIR)
            x = _out_proj(x.reshape(b * s, d), y.reshape(b * s, db), wo, tm=tm).reshape(b, s, d)
        if last and kind != 0:
            raise NotImplementedError("final norm is fused into the pooling layer only")
    return x
```

```python
import functools

import jax
import jax.numpy as jnp
from jax import lax
from jax.experimental import pallas as pl
from jax.experimental.pallas import tpu as pltpu

HEAD_DIM = 64
PAIR = 2 * HEAD_DIM
RMS_EPS = 1e-5
BAND = 128
POOL_WINDOWS = (2, 4, 8, 16)
POOL_HALO = 16
SWA_GROUP = 8
DILATIONS = (1, 4, 16)
TILE = DILATIONS[-1] * BAND
N_MIXERS = 3
LANES = 128

V7X_VMEM_LIMIT_BYTES = 56 * 1024 * 1024
NEG = -1e30
LOG2E = 1.4426950408889634

F32 = jnp.float32
BF16 = jnp.bfloat16


def _params(*semantics):
    return pltpu.CompilerParams(dimension_semantics=semantics, vmem_limit_bytes=V7X_VMEM_LIMIT_BYTES)


def _rmsnorm(x, g):
    return x * lax.rsqrt(jnp.mean(x * x, axis=-1, keepdims=True) + RMS_EPS) * g


def _silu(g):
    return g / (1.0 + jnp.exp(-g))


def _band_bias(inclusive_far_edge, at_sequence_start):
    col = lax.broadcasted_iota(jnp.int32, (2 * BAND, BAND), 0)
    row = lax.broadcasted_iota(jnp.int32, (2 * BAND, BAND), 1)
    lo = row if inclusive_far_edge else row + 1
    valid = (col >= lo) & (col <= row + BAND)
    first = valid & (col >= jnp.where(at_sequence_start, BAND, 0))
    to_bias = lambda ok: jnp.where(ok, 0.0, NEG).astype(BF16)
    return to_bias(valid), to_bias(first)


def _row_selector():
    row = lax.broadcasted_iota(jnp.int32, (BAND, BAND), 0)
    col = lax.broadcasted_iota(jnp.int32, (BAND, BAND), 1)
    return (row == col).astype(F32).astype(BF16)


def _lane_halves():
    lane = lax.broadcasted_iota(jnp.int32, (1, PAIR), 1)
    first = (lane < HEAD_DIM).astype(F32)
    return first.astype(BF16), (1.0 - first).astype(BF16), lane < HEAD_DIM


def _attend_pairs(q_pairs, k_tile, v_tile, bias, sinks=None):
    n = len(q_pairs)
    za, zb, is_a = _lane_halves()
    sel = _row_selector()
    lhs = jnp.concatenate([jnp.concatenate([q * z, sel], axis=1) for q in q_pairs for z in (za, zb)], axis=0)
    rhs = jnp.concatenate([k_tile, bias], axis=1)
    s = lax.dot_general(lhs, rhs, (((1,), (1,)), ((), ())), preferred_element_type=F32)
    s = s.reshape(2 * n, BAND, 2 * BAND)
    m = jnp.max(s, axis=-1, keepdims=True)
    if sinks is not None:
        m = jnp.concatenate([jnp.maximum(m[h], sinks[h])[None] for h in range(2 * n)], axis=0)
    e = jnp.exp2(s - m).astype(BF16)
    ones = jnp.ones((2 * BAND, 1), BF16)
    v_aug = jnp.concatenate([jnp.concatenate([v_tile * za, ones * za], axis=1),
                             jnp.concatenate([v_tile * zb, ones * zb], axis=1)], axis=0)
    e_cat = jnp.concatenate([jnp.concatenate([e[2 * p], e[2 * p + 1]], axis=1) for p in range(n)], axis=0)
    acc = jnp.dot(e_cat, v_aug, preferred_element_type=F32)
    out = []
    for p in range(n):
        num = acc[p * BAND:(p + 1) * BAND, :PAIR]
        den = acc[p * BAND:(p + 1) * BAND, PAIR:]
        m_pair = jnp.where(is_a, m[2 * p], m[2 * p + 1])
        if sinks is not None:
            den = den + jnp.exp2(jnp.where(is_a, sinks[2 * p], sinks[2 * p + 1]) - m_pair)
        out.append((num, m_pair, den))
    return out


def _proj_kernel(x_ref, g_ref, w_ref, o_ref, h_ref):
    @pl.when(pl.program_id(1) == 0)
    def _():
        h_ref[...] = _rmsnorm(x_ref[...], g_ref[...]).astype(BF16)

    o_ref[...] = jnp.dot(h_ref[...], w_ref[...], preferred_element_type=F32).astype(o_ref.dtype)


def _proj(x2d, g, w, *, tm, tn):
    t, d = x2d.shape
    e = w.shape[1]
    assert t % tm == 0 and e % tn == 0
    return pl.pallas_call(
        _proj_kernel,
        grid=(t // tm, e // tn),
        in_specs=[
            pl.BlockSpec((tm, d), lambda i, j: (i, 0)),
            pl.BlockSpec((1, d), lambda i, j: (0, 0)),
            pl.BlockSpec((d, tn), lambda i, j: (0, j)),
        ],
        out_specs=pl.BlockSpec((tm, tn), lambda i, j: (i, j)),
        out_shape=jax.ShapeDtypeStruct((t, e), BF16),
        scratch_shapes=[pltpu.VMEM((tm, d), BF16)],
        compiler_params=_params("arbitrary", "arbitrary"),
        name="in_proj",
    )(x2d, g, w)


def _matmul_kernel(a_ref, w_ref, o_ref):
    o_ref[...] = jnp.dot(a_ref[...], w_ref[...], preferred_element_type=F32).astype(o_ref.dtype)


def _matmul(a, w, *, tm, tn, n_blocks, block_of, name):
    t, d = a.shape
    assert t % tm == 0 and w.shape[1] % tn == 0
    return pl.pallas_call(
        _matmul_kernel,
        grid=(t // tm, n_blocks),
        in_specs=[pl.BlockSpec((tm, d), lambda i, j: (i, 0)), pl.BlockSpec((d, tn), lambda i, j: (0, block_of(j)))],
        out_specs=pl.BlockSpec((tm, tn), lambda i, j: (i, j)),
        out_shape=jax.ShapeDtypeStruct((t, n_blocks * tn), BF16),
        compiler_params=_params("arbitrary", "arbitrary"),
        name=name,
    )(a, w)


def _out_proj_kernel(x_ref, y_ref, w_ref, o_ref):
    o_ref[...] = x_ref[...] + jnp.dot(y_ref[...], w_ref[...], preferred_element_type=F32)


def _out_proj(x2d, y2d, w, *, tm):
    t, d = x2d.shape
    assert t % tm == 0
    return pl.pallas_call(
        _out_proj_kernel,
        grid=(t // tm,),
        in_specs=[pl.BlockSpec((tm, d), lambda i: (i, 0)),
                  pl.BlockSpec((tm, y2d.shape[1]), lambda i: (i, 0)),
                  pl.BlockSpec(w.shape, lambda i: (0, 0))],
        out_specs=pl.BlockSpec((tm, d), lambda i: (i, 0)),
        out_shape=jax.ShapeDtypeStruct(x2d.shape, F32),
        compiler_params=_params("arbitrary"),
        name="out_proj",
    )(x2d, y2d, w)


def _pool_layer_kernel(x_ref, g_ref, win_ref, wg_ref, sc_ref, wout_ref, fg_ref, o_ref, ubuf, zbuf,
                       *, tq, tsub, db, final_norm):
    j = pl.program_id(1)

    @pl.when(j == 0)
    def _():
        ubuf[0:POOL_HALO, :] = jnp.zeros((POOL_HALO, db), F32)

    @pl.when(j > 0)
    def _():
        ubuf[0:POOL_HALO, :] = ubuf[tq:tq + POOL_HALO, :]

    gd = db // len(POOL_WINDOWS)
    for r0 in range(0, tq, tsub):
        rows = slice(r0, r0 + tsub)
        x = x_ref[rows, :]
        h = _rmsnorm(x, g_ref[...]).astype(BF16)
        p = jnp.dot(h, win_ref[...], preferred_element_type=F32)
        ubuf[POOL_HALO + r0:POOL_HALO + r0 + tsub, :] = p[:, :db]
        t_abs = j * tq + r0 + lax.broadcasted_iota(jnp.int32, (tsub, 1), 0)
        for gi, w in enumerate(POOL_WINDOWS):
            cols = slice(gi * gd, (gi + 1) * gd)
            ext = ubuf[r0:r0 + POOL_HALO + tsub, cols]
            acc, span = ext, 1
            while span < w:
                acc = acc + pltpu.roll(acc, span, 0)
                span *= 2
            u, acc = ext[POOL_HALO:], acc[POOL_HALO:]
            count = jnp.minimum(t_abs + 1, w).astype(F32)
            d = (acc / count - u).astype(BF16)
            y = jnp.dot(d, wg_ref[gi], preferred_element_type=F32) * sc_ref[:, cols]
            gate = p[:, db + gi * gd:db + (gi + 1) * gd]
            zbuf[rows, cols] = (y * _silu(gate)).astype(BF16)
        out = x + jnp.dot(zbuf[rows, :], wout_ref[...], preferred_element_type=F32)
        if final_norm:
            out = _rmsnorm(out, fg_ref[...])
        o_ref[rows, :] = out


def _pool_layer(x, g, w_in, w_group, scale, w_out, final_g, *, final_norm, tq, tsub):
    b, s, d = x.shape
    db = w_out.shape[0]
    assert s % tq == 0 and tq % tsub == 0 and tsub >= POOL_HALO
    const2 = lambda bi, j: (0, 0)
    return pl.pallas_call(
        functools.partial(_pool_layer_kernel, tq=tq, tsub=tsub, db=db, final_norm=final_norm),
        grid=(b, s // tq),
        in_specs=[
            pl.BlockSpec((None, tq, d), lambda bi, j: (bi, j, 0)),
            pl.BlockSpec((1, d), const2),
            pl.BlockSpec(w_in.shape, const2),
            pl.BlockSpec(w_group.shape, lambda bi, j: (0, 0, 0)),
            pl.BlockSpec((1, db), const2),
            pl.BlockSpec(w_out.shape, const2),
            pl.BlockSpec((1, d), const2),
        ],
        out_specs=pl.BlockSpec((None, tq, d), lambda bi, j: (bi, j, 0)),
        out_shape=jax.ShapeDtypeStruct(x.shape, F32),
        scratch_shapes=[pltpu.VMEM((POOL_HALO + tq, db), F32), pltpu.VMEM((tq, db), BF16)],
        compiler_params=_params("arbitrary", "arbitrary"),
        name="pool_layer",
    )(x, g, w_in, w_group, scale, w_out, final_g)


def _swa_kernel(sink_ref, q_ref, gate_ref, kp_ref, kc_ref, vp_ref, vc_ref, x_ref, wout_ref, o_ref,
                ybuf, kdup, vdup, *, lq, n_heads):
    l = pl.program_id(1)
    base, first = _band_bias(inclusive_far_edge=False, at_sequence_start=l == 0)
    n_kv = n_heads // SWA_GROUP
    pairs_per_kv = SWA_GROUP // 2
    for j in range(n_kv):
        kcols = slice(j * HEAD_DIM, (j + 1) * HEAD_DIM)
        for src_p, src_c, dst in ((kp_ref, kc_ref, kdup), (vp_ref, vc_ref, vdup)):
            dst[j, 0:BAND, :] = jnp.concatenate([src_p[:, kcols]] * 2, axis=1)
            dst[j, BAND:BAND + lq, :] = jnp.concatenate([src_c[:, kcols]] * 2, axis=1)
    for sb in range(lq // BAND):
        rows = slice(sb * BAND, (sb + 1) * BAND)
        bias = first if sb == 0 else base
        for j in range(n_kv):
            pair_cols = [slice((j * pairs_per_kv + p) * PAIR, (j * pairs_per_kv + p + 1) * PAIR)
                         for p in range(pairs_per_kv)]
            for p, c in enumerate(pair_cols):
                (num, _, den), = _attend_pairs(
                    [q_ref[rows, c]], kdup[j, sb * BAND:(sb + 2) * BAND, :], vdup[j, sb * BAND:(sb + 2) * BAND, :],
                    bias, [sink_ref[j * SWA_GROUP + 2 * p + h] * LOG2E for h in range(2)])
                ybuf[rows, c] = (num / den * _silu(gate_ref[rows, c].astype(F32))).astype(BF16)
    o_ref[...] = x_ref[...] + jnp.dot(ybuf[...], wout_ref[...], preferred_element_type=F32)


def _swa_layer(x, p, sinks, w_out, *, lq):
    b, s, d = x.shape
    db = w_out.shape[0]
    n_heads = db // HEAD_DIM
    n_kv = n_heads // SWA_GROUP
    kv = n_kv * HEAD_DIM
    assert s % lq == 0 and lq % BAND == 0 and (2 * db) % kv == 0
    kcol = 2 * db // kv
    prev = lambda bi, l: jnp.maximum(l * (lq // BAND) - 1, 0)
    return pl.pallas_call(
        functools.partial(_swa_kernel, lq=lq, n_heads=n_heads),
        grid=(b, s // lq),
        in_specs=[
            pl.BlockSpec(memory_space=pltpu.SMEM),
            pl.BlockSpec((None, lq, db), lambda bi, l: (bi, l, 0)),
            pl.BlockSpec((None, lq, db), lambda bi, l: (bi, l, 1)),
            pl.BlockSpec((None, BAND, kv), lambda bi, l: (bi, prev(bi, l), kcol)),
            pl.BlockSpec((None, lq, kv), lambda bi, l: (bi, l, kcol)),
            pl.BlockSpec((None, BAND, kv), lambda bi, l: (bi, prev(bi, l), kcol + 1)),
            pl.BlockSpec((None, lq, kv), lambda bi, l: (bi, l, kcol + 1)),
            pl.BlockSpec((None, lq, d), lambda bi, l: (bi, l, 0)),
            pl.BlockSpec(w_out.shape, lambda bi, l: (0, 0)),
        ],
        out_specs=pl.BlockSpec((None, lq, d), lambda bi, l: (bi, l, 0)),
        out_shape=jax.ShapeDtypeStruct(x.shape, F32),
        scratch_shapes=[pltpu.VMEM((lq, db), BF16),
                        pltpu.VMEM((n_kv, BAND + lq, PAIR), BF16),
                        pltpu.VMEM((n_kv, BAND + lq, PAIR), BF16)],
        compiler_params=_params("arbitrary", "arbitrary"),
        name="swa_layer",
    )(sinks, p, p, p, p, p, p, x, w_out)


def _norm_streams_kernel(x_ref, g_ref, h0_ref, h1_ref, h2_ref, slab):
    d = x_ref.shape[1]
    ssq = jnp.zeros((TILE, 1), F32)
    for c in range(d // LANES):
        xc = x_ref[:, c * LANES:(c + 1) * LANES]
        ssq = ssq + jnp.sum(xc * xc, axis=-1, keepdims=True)
    inv = lax.rsqrt(ssq / d + RMS_EPS)
    for c in range(d // LANES):
        cols = slice(c * LANES, (c + 1) * LANES)
        hc = x_ref[:, cols] * inv * g_ref[:, cols]
        h0_ref[:, cols] = hc.astype(BF16)
        slab[c % 2] = hc
        for dil, dst in ((DILATIONS[1], h1_ref), (DILATIONS[2], h2_ref)):
            n = TILE // dil
            for r in range(dil):
                dst[r * n:(r + 1) * n, cols] = slab[c % 2, pl.ds(r, n, stride=dil), :].astype(BF16)


def _norm_streams(x, g):
    b, s, d = x.shape
    assert s % TILE == 0 and d % LANES == 0
    spec = pl.BlockSpec((None, TILE, d), lambda bi, i: (bi, i, 0))
    out = jax.ShapeDtypeStruct(x.shape, BF16)
    return pl.pallas_call(
        _norm_streams_kernel,
        grid=(b, s // TILE),
        in_specs=[spec, pl.BlockSpec((1, d), lambda bi, i: (0, 0))],
        out_specs=[spec, spec, spec],
        out_shape=[out, out, out],
        scratch_shapes=[pltpu.VMEM((2, TILE, LANES), F32)],
        compiler_params=_params("arbitrary", "arbitrary"),
        name="norm_streams",
    )(x, g)


NAT_PITCH = {1: 1, 4: 4, 16: 24}
NAT_ROWS = max(TILE // dil * pitch for dil, pitch in NAT_PITCH.items())


def _read_nat(ref, g, pr, dil):
    pitch = NAT_PITCH[dil]
    if pitch == dil:
        return ref[g, pr, 0:TILE, :]
    return ref[g, pr, 0:TILE // dil * pitch, :].reshape(TILE // dil, pitch, LANES)[:, :dil, :].reshape(TILE, LANES)


def _dilated_kernel(q0, k0, v0, gate, q1, k1, v1, q2, k2, v2, y_ref,
                    ck0, cv0, ck1, cv1, ck2, cv2, out_nat, lse_nat, *, cw):
    i = pl.program_id(2)
    groups = ((q0, k0, v0, ck0, cv0), (q1, k1, v1, ck1, cv1), (q2, k2, v2, ck2, cv2))
    base, first = _band_bias(inclusive_far_edge=True, at_sequence_start=i == 0)

    @pl.when(i == 0)
    def _():
        for _, _, _, ck, cv in groups:
            ck[...] = jnp.zeros(ck.shape, BF16)
            cv[...] = jnp.zeros(cv.shape, BF16)

    for g, (dil, (q_ref, k_ref, v_ref, ck, cv)) in enumerate(zip(DILATIONS, groups)):
        n_sb = TILE // (dil * BAND)
        for r in range(dil):
            for sb in range(n_sb):
                row0 = (r * n_sb + sb) * BAND
                for pr in range(cw // PAIR):
                    lanes = slice(pr * PAIR, (pr + 1) * PAIR)
                    if sb == 0:
                        k_t = jnp.concatenate([ck[r * BAND:(r + 1) * BAND, lanes], k_ref[row0:row0 + BAND, lanes]], axis=0)
                        v_t = jnp.concatenate([cv[r * BAND:(r + 1) * BAND, lanes], v_ref[row0:row0 + BAND, lanes]], axis=0)
                    else:
                        k_t = k_ref[row0 - BAND:row0 + BAND, lanes]
                        v_t = v_ref[row0 - BAND:row0 + BAND, lanes]
                    (num, m, den), = _attend_pairs([q_ref[row0:row0 + BAND, lanes]], k_t, v_t,
                                                   first if sb == 0 else base)
                    pitch = NAT_PITCH[dil]
                    dst = pl.ds(sb * BAND * pitch + r, BAND, stride=pitch) if dil > 1 else pl.ds(sb * BAND, BAND)
                    out_nat[g, pr, dst, :] = num / den
                    lse_nat[g, pr, dst, :] = m + jnp.log2(den)
        for r in range(dil):
            last = ((r + 1) * n_sb - 1) * BAND
            ck[r * BAND:(r + 1) * BAND, :] = k_ref[last:last + BAND, :]
            cv[r * BAND:(r + 1) * BAND, :] = v_ref[last:last + BAND, :]

    for pr in range(cw // PAIR):
        lanes = slice(pr * PAIR, (pr + 1) * PAIR)
        (l0, o0), (l1, o1), (l2, o2) = [(_read_nat(lse_nat, g, pr, dil), _read_nat(out_nat, g, pr, dil))
                                        for g, dil in enumerate(DILATIONS)]
        mx = jnp.maximum(jnp.maximum(l0, l1), l2)
        e0, e1, e2 = jnp.exp2(l0 - mx), jnp.exp2(l1 - mx), jnp.exp2(l2 - mx)
        y = (e0 * o0 + e1 * o1 + e2 * o2) / (e0 + e1 + e2)
        y_ref[:, lanes] = (y * _silu(gate[:, lanes].astype(F32))).astype(BF16)


def _dilated_attention(p0, p1, p2, *, db, cw):
    b, s, _ = p0.shape
    assert s % TILE == 0 and db % cw == 0 and cw % PAIR == 0
    nc = db // cw
    blk = lambda part: pl.BlockSpec((None, TILE, cw), lambda bi, c, i: (bi, i, part * nc + c))
    carry = lambda dil: pltpu.VMEM((dil * BAND, cw), BF16)
    nat = pltpu.VMEM((len(DILATIONS), cw // PAIR, NAT_ROWS, LANES), F32)
    return pl.pallas_call(
        functools.partial(_dilated_kernel, cw=cw),
        grid=(b, nc, s // TILE),
        in_specs=[blk(0), blk(1), blk(2), blk(3), blk(0), blk(1), blk(2), blk(0), blk(1), blk(2)],
        out_specs=pl.BlockSpec((None, TILE, cw), lambda bi, c, i: (bi, i, c)),
        out_shape=jax.ShapeDtypeStruct((b, s, db), BF16),
        scratch_shapes=[carry(DILATIONS[0]), carry(DILATIONS[0]), carry(DILATIONS[1]), carry(DILATIONS[1]),
                        carry(DILATIONS[2]), carry(DILATIONS[2]), nat, nat],
        compiler_params=_params("arbitrary", "arbitrary", "arbitrary"),
        name="dilated_attention",
    )(p0, p0, p0, p0, p1, p1, p1, p2, p2, p2)


def kernel(x, norm_g, final_g, w_out, a_w_in, a_w_group, a_scale, b_w_in, b_sinks, c_w_in):
    b, s, d = x.shape
    depth = norm_g.shape[0]
    db = w_out.shape[1]
    qscale = HEAD_DIM ** -0.5 * LOG2E
    tq = min(512, s)
    tm = min(1024, b * s)
    fg = final_g.reshape(1, d)
    for i in range(depth):
        kind, j = i % N_MIXERS, i // N_MIXERS
        g = norm_g[i].reshape(1, d)
        wo = w_out[i].astype(BF16)
        last = i == depth - 1
        if kind == 0:
            x = _pool_layer(x, g, a_w_in[j].astype(BF16), a_w_group[j].astype(BF16),
                            a_scale[j].reshape(1, db), wo, fg, final_norm=last, tq=tq, tsub=tq // 2)
        elif kind == 1:
            w = b_w_in[j]
            kvw = (w.shape[1] - 2 * db) // 2
            wq, wk, wv, wg = jnp.split(w, [db, db + kvw, db + 2 * kvw], axis=1)
            w = jnp.concatenate([wq * qscale, wg, wk, wv], axis=1).astype(BF16)
            p = _proj(x.reshape(b * s, d), g, w, tm=tm, tn=w.shape[1] // 2)
            x = _swa_layer(x, p.reshape(b, s, -1), b_sinks[j], wo, lq=tq)
        else:
            n_qkv = 3 * len(DILATIONS)
            part = jnp.arange(c_w_in.shape[2]) // db
            w = (c_w_in[j] * jnp.where((part % 3 == 0) & (part < n_qkv), qscale, 1.0)).astype(BF16)
            hs = _norm_streams(x, g)
            blocks = [(4, lambda jb: jnp.where(jb < 3, jb, n_qkv)), (3, lambda jb: jb + 3), (3, lambda jb: jb + 6)]
            ps = [_matmul(h.reshape(b * s, d), w, tm=tm, tn=db, n_blocks=nb, block_of=bof,
                          name=f"dil_proj{gi}").reshape(b, s, -1)
                  for gi, (h, (nb, bof)) in enumerate(zip(hs, blocks))]
            y = _dilated_attention(*ps, db=db, cw=2 * PAIR)
            x = _out_proj(x.reshape(b * s, d), y.reshape(b * s, db), wo, tm=tm).reshape(b, s, d)
        if last and kind != 0:
            raise NotImplementedError("final norm is fused into the pooling layer only")
    return x
```

```python
import functools

import jax
import jax.numpy as jnp
from jax import lax
from jax.experimental import pallas as pl
from jax.experimental.pallas import tpu as pltpu

HEAD_DIM = 64
PAIR = 2 * HEAD_DIM
RMS_EPS = 1e-5
BAND = 128
POOL_WINDOWS = (2, 4, 8, 16)
POOL_HALO = 16
SWA_GROUP = 8
DILATIONS = (1, 4, 16)
TILE = DILATIONS[-1] * BAND
N_MIXERS = 3
LANES = 128

V7X_VMEM_LIMIT_BYTES = 56 * 1024 * 1024
NEG = -1e30
LOG2E = 1.4426950408889634

F32 = jnp.float32
BF16 = jnp.bfloat16


def _params(*semantics):
    return pltpu.CompilerParams(dimension_semantics=semantics, vmem_limit_bytes=V7X_VMEM_LIMIT_BYTES)


def _rmsnorm(x, g):
    return x * lax.rsqrt(jnp.mean(x * x, axis=-1, keepdims=True) + RMS_EPS) * g


def _silu(g):
    return g / (1.0 + jnp.exp(-g))


def _band_bias(inclusive_far_edge, at_sequence_start):
    col = lax.broadcasted_iota(jnp.int32, (2 * BAND, BAND), 0)
    row = lax.broadcasted_iota(jnp.int32, (2 * BAND, BAND), 1)
    lo = row if inclusive_far_edge else row + 1
    valid = (col >= lo) & (col <= row + BAND)
    first = valid & (col >= jnp.where(at_sequence_start, BAND, 0))
    to_bias = lambda ok: jnp.where(ok, 0.0, NEG).astype(BF16)
    return to_bias(valid), to_bias(first)


def _row_selector():
    row = lax.broadcasted_iota(jnp.int32, (BAND, BAND), 0)
    col = lax.broadcasted_iota(jnp.int32, (BAND, BAND), 1)
    return (row == col).astype(F32).astype(BF16)


def _lane_halves():
    lane = lax.broadcasted_iota(jnp.int32, (1, PAIR), 1)
    first = (lane < HEAD_DIM).astype(F32)
    return first.astype(BF16), (1.0 - first).astype(BF16), lane < HEAD_DIM


def _attend_pairs(q_pairs, k_tile, v_tile, bias, sinks=None):
    n = len(q_pairs)
    za, zb, is_a = _lane_halves()
    sel = _row_selector()
    lhs = jnp.concatenate([jnp.concatenate([q * z, sel], axis=1) for q in q_pairs for z in (za, zb)], axis=0)
    rhs = jnp.concatenate([k_tile, bias], axis=1)
    s = lax.dot_general(lhs, rhs, (((1,), (1,)), ((), ())), preferred_element_type=F32)
    s = s.reshape(2 * n, BAND, 2 * BAND)
    m = jnp.max(s, axis=-1, keepdims=True)
    if sinks is not None:
        m = jnp.concatenate([jnp.maximum(m[h], sinks[h])[None] for h in range(2 * n)], axis=0)
    e = jnp.exp2(s - m).astype(BF16)
    ones = jnp.ones((2 * BAND, 1), BF16)
    v_aug = jnp.concatenate([jnp.concatenate([v_tile * za, ones * za], axis=1),
                             jnp.concatenate([v_tile * zb, ones * zb], axis=1)], axis=0)
    e_cat = jnp.concatenate([jnp.concatenate([e[2 * p], e[2 * p + 1]], axis=1) for p in range(n)], axis=0)
    acc = jnp.dot(e_cat, v_aug, preferred_element_type=F32)
    out = []
    for p in range(n):
        num = acc[p * BAND:(p + 1) * BAND, :PAIR]
        den = acc[p * BAND:(p + 1) * BAND, PAIR:]
        m_pair = jnp.where(is_a, m[2 * p], m[2 * p + 1])
        if sinks is not None:
            den = den + jnp.exp2(jnp.where(is_a, sinks[2 * p], sinks[2 * p + 1]) - m_pair)
        out.append((num, m_pair, den))
    return out


def _proj_kernel(x_ref, g_ref, w_ref, o_ref, *, tsub):
    for r0 in range(0, x_ref.shape[0], tsub):
        h = _rmsnorm(x_ref[r0:r0 + tsub, :], g_ref[...]).astype(BF16)
        o_ref[r0:r0 + tsub, :] = jnp.dot(h, w_ref[...], preferred_element_type=F32).astype(o_ref.dtype)


def _proj(x2d, g, w, *, tm, tsub):
    t, d = x2d.shape
    e = w.shape[1]
    assert t % tm == 0 and tm % tsub == 0
    return pl.pallas_call(
        functools.partial(_proj_kernel, tsub=tsub),
        grid=(t // tm,),
        in_specs=[
            pl.BlockSpec((tm, d), lambda i: (i, 0)),
            pl.BlockSpec((1, d), lambda i: (0, 0)),
            pl.BlockSpec((d, e), lambda i: (0, 0), pipeline_mode=pl.Buffered(1)),
        ],
        out_specs=pl.BlockSpec((tm, e), lambda i: (i, 0)),
        out_shape=jax.ShapeDtypeStruct((t, e), BF16),
        compiler_params=_params("arbitrary"),
        name="in_proj",
    )(x2d, g, w)


def _matmul_kernel(a_ref, w_ref, o_ref):
    o_ref[...] = jnp.dot(a_ref[...], w_ref[...], preferred_element_type=F32).astype(o_ref.dtype)


def _matmul(a, w, *, tm, tn, n_blocks, block_of, name):
    t, d = a.shape
    assert t % tm == 0 and w.shape[1] % tn == 0
    return pl.pallas_call(
        _matmul_kernel,
        grid=(t // tm, n_blocks),
        in_specs=[pl.BlockSpec((tm, d), lambda i, j: (i, 0)), pl.BlockSpec((d, tn), lambda i, j: (0, block_of(j)))],
        out_specs=pl.BlockSpec((tm, tn), lambda i, j: (i, j)),
        out_shape=jax.ShapeDtypeStruct((t, n_blocks * tn), BF16),
        compiler_params=_params("arbitrary", "arbitrary"),
        name=name,
    )(a, w)


def _out_proj_kernel(x_ref, y_ref, w_ref, o_ref):
    o_ref[...] = x_ref[...] + jnp.dot(y_ref[...], w_ref[...], preferred_element_type=F32)


def _out_proj(x2d, y2d, w, *, tm):
    t, d = x2d.shape
    assert t % tm == 0
    return pl.pallas_call(
        _out_proj_kernel,
        grid=(t // tm,),
        in_specs=[pl.BlockSpec((tm, d), lambda i: (i, 0)),
                  pl.BlockSpec((tm, y2d.shape[1]), lambda i: (i, 0)),
                  pl.BlockSpec(w.shape, lambda i: (0, 0))],
        out_specs=pl.BlockSpec((tm, d), lambda i: (i, 0)),
        out_shape=jax.ShapeDtypeStruct(x2d.shape, F32),
        compiler_params=_params("arbitrary"),
        name="out_proj",
    )(x2d, y2d, w)


def _pool_layer_kernel(x_ref, g_ref, win_ref, wg_ref, sc_ref, wout_ref, fg_ref, o_ref, ubuf, zbuf,
                       *, tq, tsub, db, final_norm):
    j = pl.program_id(1)

    @pl.when(j == 0)
    def _():
        ubuf[0:POOL_HALO, :] = jnp.zeros((POOL_HALO, db), F32)

    @pl.when(j > 0)
    def _():
        ubuf[0:POOL_HALO, :] = ubuf[tq:tq + POOL_HALO, :]

    gd = db // len(POOL_WINDOWS)
    for r0 in range(0, tq, tsub):
        rows = slice(r0, r0 + tsub)
        x = x_ref[rows, :]
        h = _rmsnorm(x, g_ref[...]).astype(BF16)
        p = jnp.dot(h, win_ref[...], preferred_element_type=F32)
        ubuf[POOL_HALO + r0:POOL_HALO + r0 + tsub, :] = p[:, :db]
        t_abs = j * tq + r0 + lax.broadcasted_iota(jnp.int32, (tsub, 1), 0)
        for gi, w in enumerate(POOL_WINDOWS):
            cols = slice(gi * gd, (gi + 1) * gd)
            ext = ubuf[r0:r0 + POOL_HALO + tsub, cols]
            acc, span = ext, 1
            while span < w:
                acc = acc + pltpu.roll(acc, span, 0)
                span *= 2
            u, acc = ext[POOL_HALO:], acc[POOL_HALO:]
            count = jnp.minimum(t_abs + 1, w).astype(F32)
            d = (acc / count - u).astype(BF16)
            y = jnp.dot(d, wg_ref[gi], preferred_element_type=F32) * sc_ref[:, cols]
            gate = p[:, db + gi * gd:db + (gi + 1) * gd]
            zbuf[rows, cols] = (y * _silu(gate)).astype(BF16)
        out = x + jnp.dot(zbuf[rows, :], wout_ref[...], preferred_element_type=F32)
        if final_norm:
            out = _rmsnorm(out, fg_ref[...])
        o_ref[rows, :] = out


def _pool_layer(x, g, w_in, w_group, scale, w_out, final_g, *, final_norm, tq, tsub):
    b, s, d = x.shape
    db = w_out.shape[0]
    assert s % tq == 0 and tq % tsub == 0 and tsub >= POOL_HALO
    const2 = lambda bi, j: (0, 0)
    return pl.pallas_call(
        functools.partial(_pool_layer_kernel, tq=tq, tsub=tsub, db=db, final_norm=final_norm),
        grid=(b, s // tq),
        in_specs=[
            pl.BlockSpec((None, tq, d), lambda bi, j: (bi, j, 0)),
            pl.BlockSpec((1, d), const2),
            pl.BlockSpec(w_in.shape, const2, pipeline_mode=pl.Buffered(1)),
            pl.BlockSpec(w_group.shape, lambda bi, j: (0, 0, 0), pipeline_mode=pl.Buffered(1)),
            pl.BlockSpec((1, db), const2),
            pl.BlockSpec(w_out.shape, const2, pipeline_mode=pl.Buffered(1)),
            pl.BlockSpec((1, d), const2),
        ],
        out_specs=pl.BlockSpec((None, tq, d), lambda bi, j: (bi, j, 0)),
        out_shape=jax.ShapeDtypeStruct(x.shape, F32),
        scratch_shapes=[pltpu.VMEM((POOL_HALO + tq, db), F32), pltpu.VMEM((tq, db), BF16)],
        compiler_params=_params("arbitrary", "arbitrary"),
        name="pool_layer",
    )(x, g, w_in, w_group, scale, w_out, final_g)


def _swa_kernel(sink_ref, q_ref, gate_ref, kp_ref, kc_ref, vp_ref, vc_ref, x_ref, wout_ref, o_ref,
                ybuf, kdup, vdup, *, lq, n_heads):
    l = pl.program_id(1)
    base, first = _band_bias(inclusive_far_edge=False, at_sequence_start=l == 0)
    n_kv = n_heads // SWA_GROUP
    pairs_per_kv = SWA_GROUP // 2
    for j in range(n_kv):
        kcols = slice(j * HEAD_DIM, (j + 1) * HEAD_DIM)
        for src_p, src_c, dst in ((kp_ref, kc_ref, kdup), (vp_ref, vc_ref, vdup)):
            dst[j, 0:BAND, :] = jnp.concatenate([src_p[:, kcols]] * 2, axis=1)
            dst[j, BAND:BAND + lq, :] = jnp.concatenate([src_c[:, kcols]] * 2, axis=1)
    for sb in range(lq // BAND):
        rows = slice(sb * BAND, (sb + 1) * BAND)
        bias = first if sb == 0 else base
        for j in range(n_kv):
            pair_cols = [slice((j * pairs_per_kv + p) * PAIR, (j * pairs_per_kv + p + 1) * PAIR)
                         for p in range(pairs_per_kv)]
            for p, c in enumerate(pair_cols):
                (num, _, den), = _attend_pairs(
                    [q_ref[rows, c]], kdup[j, sb * BAND:(sb + 2) * BAND, :], vdup[j, sb * BAND:(sb + 2) * BAND, :],
                    bias, [sink_ref[j * SWA_GROUP + 2 * p + h] * LOG2E for h in range(2)])
                ybuf[rows, c] = (num / den * _silu(gate_ref[rows, c].astype(F32))).astype(BF16)
    o_ref[...] = x_ref[...] + jnp.dot(ybuf[...], wout_ref[...], preferred_element_type=F32)


def _swa_layer(x, p, sinks, w_out, *, lq):
    b, s, d = x.shape
    db = w_out.shape[0]
    n_heads = db // HEAD_DIM
    n_kv = n_heads // SWA_GROUP
    kv = n_kv * HEAD_DIM
    assert s % lq == 0 and lq % BAND == 0 and (2 * db) % kv == 0
    kcol = 2 * db // kv
    prev = lambda bi, l: jnp.maximum(l * (lq // BAND) - 1, 0)
    return pl.pallas_call(
        functools.partial(_swa_kernel, lq=lq, n_heads=n_heads),
        grid=(b, s // lq),
        in_specs=[
            pl.BlockSpec(memory_space=pltpu.SMEM),
            pl.BlockSpec((None, lq, db), lambda bi, l: (bi, l, 0)),
            pl.BlockSpec((None, lq, db), lambda bi, l: (bi, l, 1)),
            pl.BlockSpec((None, BAND, kv), lambda bi, l: (bi, prev(bi, l), kcol)),
            pl.BlockSpec((None, lq, kv), lambda bi, l: (bi, l, kcol)),
            pl.BlockSpec((None, BAND, kv), lambda bi, l: (bi, prev(bi, l), kcol + 1)),
            pl.BlockSpec((None, lq, kv), lambda bi, l: (bi, l, kcol + 1)),
            pl.BlockSpec((None, lq, d), lambda bi, l: (bi, l, 0)),
            pl.BlockSpec(w_out.shape, lambda bi, l: (0, 0)),
        ],
        out_specs=pl.BlockSpec((None, lq, d), lambda bi, l: (bi, l, 0)),
        out_shape=jax.ShapeDtypeStruct(x.shape, F32),
        scratch_shapes=[pltpu.VMEM((lq, db), BF16),
                        pltpu.VMEM((n_kv, BAND + lq, PAIR), BF16),
                        pltpu.VMEM((n_kv, BAND + lq, PAIR), BF16)],
        compiler_params=_params("arbitrary", "arbitrary"),
        name="swa_layer",
    )(sinks, p, p, p, p, p, p, x, w_out)


def _norm_streams_kernel(x_ref, g_ref, h0_ref, h1_ref, h2_ref, slab):
    d = x_ref.shape[1]
    ssq = jnp.zeros((TILE, 1), F32)
    for c in range(d // LANES):
        xc = x_ref[:, c * LANES:(c + 1) * LANES]
        ssq = ssq + jnp.sum(xc * xc, axis=-1, keepdims=True)
    inv = lax.rsqrt(ssq / d + RMS_EPS)
    for c in range(d // LANES):
        cols = slice(c * LANES, (c + 1) * LANES)
        hc = x_ref[:, cols] * inv * g_ref[:, cols]
        h0_ref[:, cols] = hc.astype(BF16)
        slab[c % 2] = hc
        for dil, dst in ((DILATIONS[1], h1_ref), (DILATIONS[2], h2_ref)):
            n = TILE // dil
            for r in range(dil):
                dst[r * n:(r + 1) * n, cols] = slab[c % 2, pl.ds(r, n, stride=dil), :].astype(BF16)


def _norm_streams(x, g):
    b, s, d = x.shape
    assert s % TILE == 0 and d % LANES == 0
    spec = pl.BlockSpec((None, TILE, d), lambda bi, i: (bi, i, 0))
    out = jax.ShapeDtypeStruct(x.shape, BF16)
    return pl.pallas_call(
        _norm_streams_kernel,
        grid=(b, s // TILE),
        in_specs=[spec, pl.BlockSpec((1, d), lambda bi, i: (0, 0))],
        out_specs=[spec, spec, spec],
        out_shape=[out, out, out],
        scratch_shapes=[pltpu.VMEM((2, TILE, LANES), F32)],
        compiler_params=_params("arbitrary", "arbitrary"),
        name="norm_streams",
    )(x, g)


NAT_PITCH = {1: 1, 4: 4, 16: 24}
NAT_ROWS = max(TILE // dil * pitch for dil, pitch in NAT_PITCH.items())


def _read_nat(ref, g, pr, dil):
    pitch = NAT_PITCH[dil]
    if pitch == dil:
        return ref[g, pr, 0:TILE, :]
    return ref[g, pr, 0:TILE // dil * pitch, :].reshape(TILE // dil, pitch, LANES)[:, :dil, :].reshape(TILE, LANES)


def _dilated_kernel(q0, k0, v0, gate, q1, k1, v1, q2, k2, v2, y_ref,
                    ck0, cv0, ck1, cv1, ck2, cv2, out_nat, lse_nat, *, cw):
    i = pl.program_id(2)
    groups = ((q0, k0, v0, ck0, cv0), (q1, k1, v1, ck1, cv1), (q2, k2, v2, ck2, cv2))
    base, first = _band_bias(inclusive_far_edge=True, at_sequence_start=i == 0)

    @pl.when(i == 0)
    def _():
        for _, _, _, ck, cv in groups:
            ck[...] = jnp.zeros(ck.shape, BF16)
            cv[...] = jnp.zeros(cv.shape, BF16)

    for g, (dil, (q_ref, k_ref, v_ref, ck, cv)) in enumerate(zip(DILATIONS, groups)):
        n_sb = TILE // (dil * BAND)
        for r in range(dil):
            for sb in range(n_sb):
                row0 = (r * n_sb + sb) * BAND
                for pr in range(cw // PAIR):
                    lanes = slice(pr * PAIR, (pr + 1) * PAIR)
                    if sb == 0:
                        k_t = jnp.concatenate([ck[r * BAND:(r + 1) * BAND, lanes], k_ref[row0:row0 + BAND, lanes]], axis=0)
                        v_t = jnp.concatenate([cv[r * BAND:(r + 1) * BAND, lanes], v_ref[row0:row0 + BAND, lanes]], axis=0)
                    else:
                        k_t = k_ref[row0 - BAND:row0 + BAND, lanes]
                        v_t = v_ref[row0 - BAND:row0 + BAND, lanes]
                    (num, m, den), = _attend_pairs([q_ref[row0:row0 + BAND, lanes]], k_t, v_t,
                                                   first if sb == 0 else base)
                    pitch = NAT_PITCH[dil]
                    dst = pl.ds(sb * BAND * pitch + r, BAND, stride=pitch) if dil > 1 else pl.ds(sb * BAND, BAND)
                    out_nat[g, pr, dst, :] = num / den
                    lse_nat[g, pr, dst, :] = m + jnp.log2(den)
        for r in range(dil):
            last = ((r + 1) * n_sb - 1) * BAND
            ck[r * BAND:(r + 1) * BAND, :] = k_ref[last:last + BAND, :]
            cv[r * BAND:(r + 1) * BAND, :] = v_ref[last:last + BAND, :]

    for pr in range(cw // PAIR):
        lanes = slice(pr * PAIR, (pr + 1) * PAIR)
        (l0, o0), (l1, o1), (l2, o2) = [(_read_nat(lse_nat, g, pr, dil), _read_nat(out_nat, g, pr, dil))
                                        for g, dil in enumerate(DILATIONS)]
        mx = jnp.maximum(jnp.maximum(l0, l1), l2)
        e0, e1, e2 = jnp.exp2(l0 - mx), jnp.exp2(l1 - mx), jnp.exp2(l2 - mx)
        y = (e0 * o0 + e1 * o1 + e2 * o2) / (e0 + e1 + e2)
        y_ref[:, lanes] = (y * _silu(gate[:, lanes].astype(F32))).astype(BF16)


def _dilated_attention(p0, p1, p2, *, db, cw):
    b, s, _ = p0.shape
    assert s % TILE == 0 and db % cw == 0 and cw % PAIR == 0
    nc = db // cw
    blk = lambda part: pl.BlockSpec((None, TILE, cw), lambda bi, c, i: (bi, i, part * nc + c))
    carry = lambda dil: pltpu.VMEM((dil * BAND, cw), BF16)
    nat = pltpu.VMEM((len(DILATIONS), cw // PAIR, NAT_ROWS, LANES), F32)
    return pl.pallas_call(
        functools.partial(_dilated_kernel, cw=cw),
        grid=(b, nc, s // TILE),
        in_specs=[blk(0), blk(1), blk(2), blk(3), blk(0), blk(1), blk(2), blk(0), blk(1), blk(2)],
        out_specs=pl.BlockSpec((None, TILE, cw), lambda bi, c, i: (bi, i, c)),
        out_shape=jax.ShapeDtypeStruct((b, s, db), BF16),
        scratch_shapes=[carry(DILATIONS[0]), carry(DILATIONS[0]), carry(DILATIONS[1]), carry(DILATIONS[1]),
                        carry(DILATIONS[2]), carry(DILATIONS[2]), nat, nat],
        compiler_params=_params("arbitrary", "arbitrary", "arbitrary"),
        name="dilated_attention",
    )(p0, p0, p0, p0, p1, p1, p1, p2, p2, p2)


def kernel(x, norm_g, final_g, w_out, a_w_in, a_w_group, a_scale, b_w_in, b_sinks, c_w_in):
    b, s, d = x.shape
    depth = norm_g.shape[0]
    db = w_out.shape[1]
    qscale = HEAD_DIM ** -0.5 * LOG2E
    tq = min(512, s)
    tm = min(1024, b * s)
    fg = final_g.reshape(1, d)
    for i in range(depth):
        kind, j = i % N_MIXERS, i // N_MIXERS
        g = norm_g[i].reshape(1, d)
        wo = w_out[i].astype(BF16)
        last = i == depth - 1
        if kind == 0:
            x = _pool_layer(x, g, a_w_in[j].astype(BF16), a_w_group[j].astype(BF16),
                            a_scale[j].reshape(1, db), wo, fg, final_norm=last, tq=min(2 * tq, s), tsub=tq)
        elif kind == 1:
            w = b_w_in[j]
            kvw = (w.shape[1] - 2 * db) // 2
            wq, wk, wv, wg = jnp.split(w, [db, db + kvw, db + 2 * kvw], axis=1)
            w = jnp.concatenate([wq * qscale, wg, wk, wv], axis=1).astype(BF16)
            p = _proj(x.reshape(b * s, d), g, w, tm=tm, tsub=tm // 2)
            x = _swa_layer(x, p.reshape(b, s, -1), b_sinks[j], wo, lq=tq)
        else:
            n_qkv = 3 * len(DILATIONS)
            part = jnp.arange(c_w_in.shape[2]) // db
            w = (c_w_in[j] * jnp.where((part % 3 == 0) & (part < n_qkv), qscale, 1.0)).astype(BF16)
            hs = _norm_streams(x, g)
            blocks = [(4, lambda jb: jnp.where(jb < 3, jb, n_qkv)), (3, lambda jb: jb + 3), (3, lambda jb: jb + 6)]
            ps = [_matmul(h.reshape(b * s, d), w, tm=tm, tn=db, n_blocks=nb, block_of=bof,
                          name=f"dil_proj{gi}").reshape(b, s, -1)
                  for gi, (h, (nb, bof)) in enumerate(zip(hs, blocks))]
            y = _dilated_attention(*ps, db=db, cw=2 * PAIR)
            x = _out_proj(x.reshape(b * s, d), y.reshape(b * s, db), wo, tm=tm).reshape(b, s, d)
        if last and kind != 0:
            raise NotImplementedError("final norm is fused into the pooling layer only")
    return x
```

```python
import functools

import jax
import jax.numpy as jnp
from jax import lax
from jax.experimental import pallas as pl
from jax.experimental.pallas import tpu as pltpu

HEAD_DIM = 64
PAIR = 2 * HEAD_DIM
RMS_EPS = 1e-5
BAND = 128
POOL_WINDOWS = (2, 4, 8, 16)
POOL_HALO = 16
SWA_GROUP = 8
DILATIONS = (1, 4, 16)
TILE = DILATIONS[-1] * BAND
N_MIXERS = 3
LANES = 128

V7X_VMEM_LIMIT_BYTES = 56 * 1024 * 1024
NEG = -1e30
LOG2E = 1.4426950408889634

F32 = jnp.float32
BF16 = jnp.bfloat16


def _params(*semantics):
    return pltpu.CompilerParams(dimension_semantics=semantics, vmem_limit_bytes=V7X_VMEM_LIMIT_BYTES)


def _rmsnorm(x, g):
    return x * lax.rsqrt(jnp.mean(x * x, axis=-1, keepdims=True) + RMS_EPS) * g


def _silu(g):
    return g / (1.0 + jnp.exp(-g))


def _band_bias(inclusive_far_edge, at_sequence_start):
    col = lax.broadcasted_iota(jnp.int32, (2 * BAND, BAND), 0)
    row = lax.broadcasted_iota(jnp.int32, (2 * BAND, BAND), 1)
    lo = row if inclusive_far_edge else row + 1
    valid = (col >= lo) & (col <= row + BAND)
    first = valid & (col >= jnp.where(at_sequence_start, BAND, 0))
    to_bias = lambda ok: jnp.where(ok, 0.0, NEG).astype(BF16)
    return to_bias(valid), to_bias(first)


def _row_selector():
    row = lax.broadcasted_iota(jnp.int32, (BAND, BAND), 0)
    col = lax.broadcasted_iota(jnp.int32, (BAND, BAND), 1)
    return (row == col).astype(F32).astype(BF16)


def _lane_halves():
    lane = lax.broadcasted_iota(jnp.int32, (1, PAIR), 1)
    first = (lane < HEAD_DIM).astype(F32)
    return first.astype(BF16), (1.0 - first).astype(BF16), lane < HEAD_DIM


def _attend_pairs(q_pairs, k_tile, v_tile, bias, sinks=None):
    n = len(q_pairs)
    za, zb, is_a = _lane_halves()
    sel = _row_selector()
    lhs = jnp.concatenate([jnp.concatenate([q * z, sel], axis=1) for q in q_pairs for z in (za, zb)], axis=0)
    rhs = jnp.concatenate([k_tile, bias], axis=1)
    s = lax.dot_general(lhs, rhs, (((1,), (1,)), ((), ())), preferred_element_type=F32)
    s = s.reshape(2 * n, BAND, 2 * BAND)
    m = jnp.max(s, axis=-1, keepdims=True)
    if sinks is not None:
        m = jnp.concatenate([jnp.maximum(m[h], sinks[h])[None] for h in range(2 * n)], axis=0)
    e = jnp.exp2(s - m).astype(BF16)
    ones = jnp.ones((2 * BAND, 1), BF16)
    v_aug = jnp.concatenate([jnp.concatenate([v_tile * za, ones * za], axis=1),
                             jnp.concatenate([v_tile * zb, ones * zb], axis=1)], axis=0)
    e_cat = jnp.concatenate([jnp.concatenate([e[2 * p], e[2 * p + 1]], axis=1) for p in range(n)], axis=0)
    acc = jnp.dot(e_cat, v_aug, preferred_element_type=F32)
    out = []
    for p in range(n):
        num = acc[p * BAND:(p + 1) * BAND, :PAIR]
        den = acc[p * BAND:(p + 1) * BAND, PAIR:]
        m_pair = jnp.where(is_a, m[2 * p], m[2 * p + 1])
        if sinks is not None:
            den = den + jnp.exp2(jnp.where(is_a, sinks[2 * p], sinks[2 * p + 1]) - m_pair)
        out.append((num, m_pair, den))
    return out


def _proj_kernel(x_ref, g_ref, w_ref, o_ref, *, tsub):
    for r0 in range(0, x_ref.shape[0], tsub):
        h = _rmsnorm(x_ref[r0:r0 + tsub, :], g_ref[...]).astype(BF16)
        o_ref[r0:r0 + tsub, :] = jnp.dot(h, w_ref[...], preferred_element_type=F32).astype(o_ref.dtype)


def _proj(x2d, g, w, *, tm, tsub):
    t, d = x2d.shape
    e = w.shape[1]
    assert t % tm == 0 and tm % tsub == 0
    return pl.pallas_call(
        functools.partial(_proj_kernel, tsub=tsub),
        grid=(t // tm,),
        in_specs=[
            pl.BlockSpec((tm, d), lambda i: (i, 0)),
            pl.BlockSpec((1, d), lambda i: (0, 0)),
            pl.BlockSpec((d, e), lambda i: (0, 0), pipeline_mode=pl.Buffered(1)),
        ],
        out_specs=pl.BlockSpec((tm, e), lambda i: (i, 0)),
        out_shape=jax.ShapeDtypeStruct((t, e), BF16),
        compiler_params=_params("arbitrary"),
        name="in_proj",
    )(x2d, g, w)


def _matmul_kernel(a_ref, w_ref, o_ref):
    o_ref[...] = jnp.dot(a_ref[...], w_ref[...], preferred_element_type=F32).astype(o_ref.dtype)


def _matmul(a, w, *, tm, tn, n_blocks, block_of, name):
    t, d = a.shape
    assert t % tm == 0 and w.shape[1] % tn == 0
    return pl.pallas_call(
        _matmul_kernel,
        grid=(t // tm, n_blocks),
        in_specs=[pl.BlockSpec((tm, d), lambda i, j: (i, 0)), pl.BlockSpec((d, tn), lambda i, j: (0, block_of(j)))],
        out_specs=pl.BlockSpec((tm, tn), lambda i, j: (i, j)),
        out_shape=jax.ShapeDtypeStruct((t, n_blocks * tn), BF16),
        compiler_params=_params("arbitrary", "arbitrary"),
        name=name,
    )(a, w)


def _out_proj_kernel(x_ref, y_ref, w_ref, o_ref):
    o_ref[...] = x_ref[...] + jnp.dot(y_ref[...], w_ref[...], preferred_element_type=F32)


def _out_proj(x2d, y2d, w, *, tm):
    t, d = x2d.shape
    assert t % tm == 0
    return pl.pallas_call(
        _out_proj_kernel,
        grid=(t // tm,),
        in_specs=[pl.BlockSpec((tm, d), lambda i: (i, 0)),
                  pl.BlockSpec((tm, y2d.shape[1]), lambda i: (i, 0)),
                  pl.BlockSpec(w.shape, lambda i: (0, 0))],
        out_specs=pl.BlockSpec((tm, d), lambda i: (i, 0)),
        out_shape=jax.ShapeDtypeStruct(x2d.shape, F32),
        compiler_params=_params("arbitrary"),
        name="out_proj",
    )(x2d, y2d, w)


def _pool_layer_kernel(x_ref, g_ref, win_ref, wg_ref, sc_ref, wout_ref, fg_ref, o_ref, ubuf, zbuf,
                       *, tq, tsub, db, final_norm):
    j = pl.program_id(1)

    @pl.when(j == 0)
    def _():
        ubuf[0:POOL_HALO, :] = jnp.zeros((POOL_HALO, db), F32)

    @pl.when(j > 0)
    def _():
        ubuf[0:POOL_HALO, :] = ubuf[tq:tq + POOL_HALO, :]

    gd = db // len(POOL_WINDOWS)
    for r0 in range(0, tq, tsub):
        rows = slice(r0, r0 + tsub)
        x = x_ref[rows, :]
        h = _rmsnorm(x, g_ref[...]).astype(BF16)
        p = jnp.dot(h, win_ref[...], preferred_element_type=F32)
        ubuf[POOL_HALO + r0:POOL_HALO + r0 + tsub, :] = p[:, :db]
        t_abs = j * tq + r0 + lax.broadcasted_iota(jnp.int32, (tsub, 1), 0)
        for gi, w in enumerate(POOL_WINDOWS):
            cols = slice(gi * gd, (gi + 1) * gd)
            ext = ubuf[r0:r0 + POOL_HALO + tsub, cols]
            acc, span = ext, 1
            while span < w:
                acc = acc + pltpu.roll(acc, span, 0)
                span *= 2
            u, acc = ext[POOL_HALO:], acc[POOL_HALO:]
            count = jnp.minimum(t_abs + 1, w).astype(F32)
            d = (acc / count - u).astype(BF16)
            y = jnp.dot(d, wg_ref[gi], preferred_element_type=F32) * sc_ref[:, cols]
            gate = p[:, db + gi * gd:db + (gi + 1) * gd]
            zbuf[rows, cols] = (y * _silu(gate)).astype(BF16)
        out = x + jnp.dot(zbuf[rows, :], wout_ref[...], preferred_element_type=F32)
        if final_norm:
            out = _rmsnorm(out, fg_ref[...])
        o_ref[rows, :] = out


def _pool_layer(x, g, w_in, w_group, scale, w_out, final_g, *, final_norm, tq, tsub):
    b, s, d = x.shape
    db = w_out.shape[0]
    assert s % tq == 0 and tq % tsub == 0 and tsub >= POOL_HALO
    const2 = lambda bi, j: (0, 0)
    return pl.pallas_call(
        functools.partial(_pool_layer_kernel, tq=tq, tsub=tsub, db=db, final_norm=final_norm),
        grid=(b, s // tq),
        in_specs=[
            pl.BlockSpec((None, tq, d), lambda bi, j: (bi, j, 0)),
            pl.BlockSpec((1, d), const2),
            pl.BlockSpec(w_in.shape, const2, pipeline_mode=pl.Buffered(1)),
            pl.BlockSpec(w_group.shape, lambda bi, j: (0, 0, 0), pipeline_mode=pl.Buffered(1)),
            pl.BlockSpec((1, db), const2),
            pl.BlockSpec(w_out.shape, const2, pipeline_mode=pl.Buffered(1)),
            pl.BlockSpec((1, d), const2),
        ],
        out_specs=pl.BlockSpec((None, tq, d), lambda bi, j: (bi, j, 0)),
        out_shape=jax.ShapeDtypeStruct(x.shape, F32),
        scratch_shapes=[pltpu.VMEM((POOL_HALO + tq, db), F32), pltpu.VMEM((tq, db), BF16)],
        compiler_params=_params("arbitrary", "arbitrary"),
        name="pool_layer",
    )(x, g, w_in, w_group, scale, w_out, final_g)


def _swa_kernel(sink_ref, q_ref, gate_ref, kp_ref, kc_ref, vp_ref, vc_ref, x_ref, wout_ref, o_ref,
                ybuf, kdup, vdup, *, lq, n_heads):
    l = pl.program_id(1)
    base, first = _band_bias(inclusive_far_edge=False, at_sequence_start=l == 0)
    n_kv = n_heads // SWA_GROUP
    pairs_per_kv = SWA_GROUP // 2
    for j in range(n_kv):
        kcols = slice(j * HEAD_DIM, (j + 1) * HEAD_DIM)
        for src_p, src_c, dst in ((kp_ref, kc_ref, kdup), (vp_ref, vc_ref, vdup)):
            dst[j, 0:BAND, :] = jnp.concatenate([src_p[:, kcols]] * 2, axis=1)
            dst[j, BAND:BAND + lq, :] = jnp.concatenate([src_c[:, kcols]] * 2, axis=1)
    for sb in range(lq // BAND):
        rows = slice(sb * BAND, (sb + 1) * BAND)
        bias = first if sb == 0 else base
        for j in range(n_kv):
            pair_cols = [slice((j * pairs_per_kv + p) * PAIR, (j * pairs_per_kv + p + 1) * PAIR)
                         for p in range(pairs_per_kv)]
            for p, c in enumerate(pair_cols):
                (num, _, den), = _attend_pairs(
                    [q_ref[rows, c]], kdup[j, sb * BAND:(sb + 2) * BAND, :], vdup[j, sb * BAND:(sb + 2) * BAND, :],
                    bias, [sink_ref[j * SWA_GROUP + 2 * p + h] * LOG2E for h in range(2)])
                ybuf[rows, c] = (num / den * _silu(gate_ref[rows, c].astype(F32))).astype(BF16)
    o_ref[...] = x_ref[...] + jnp.dot(ybuf[...], wout_ref[...], preferred_element_type=F32)


def _swa_layer(x, p, sinks, w_out, *, lq):
    b, s, d = x.shape
    db = w_out.shape[0]
    n_heads = db // HEAD_DIM
    n_kv = n_heads // SWA_GROUP
    kv = n_kv * HEAD_DIM
    assert s % lq == 0 and lq % BAND == 0 and (2 * db) % kv == 0
    kcol = 2 * db // kv
    prev = lambda bi, l: jnp.maximum(l * (lq // BAND) - 1, 0)
    return pl.pallas_call(
        functools.partial(_swa_kernel, lq=lq, n_heads=n_heads),
        grid=(b, s // lq),
        in_specs=[
            pl.BlockSpec(memory_space=pltpu.SMEM),
            pl.BlockSpec((None, lq, db), lambda bi, l: (bi, l, 0)),
            pl.BlockSpec((None, lq, db), lambda bi, l: (bi, l, 1)),
            pl.BlockSpec((None, BAND, kv), lambda bi, l: (bi, prev(bi, l), kcol)),
            pl.BlockSpec((None, lq, kv), lambda bi, l: (bi, l, kcol)),
            pl.BlockSpec((None, BAND, kv), lambda bi, l: (bi, prev(bi, l), kcol + 1)),
            pl.BlockSpec((None, lq, kv), lambda bi, l: (bi, l, kcol + 1)),
            pl.BlockSpec((None, lq, d), lambda bi, l: (bi, l, 0)),
            pl.BlockSpec(w_out.shape, lambda bi, l: (0, 0)),
        ],
        out_specs=pl.BlockSpec((None, lq, d), lambda bi, l: (bi, l, 0)),
        out_shape=jax.ShapeDtypeStruct(x.shape, F32),
        scratch_shapes=[pltpu.VMEM((lq, db), BF16),
                        pltpu.VMEM((n_kv, BAND + lq, PAIR), BF16),
                        pltpu.VMEM((n_kv, BAND + lq, PAIR), BF16)],
        compiler_params=_params("arbitrary", "arbitrary"),
        name="swa_layer",
    )(sinks, p, p, p, p, p, p, x, w_out)


NAT_PITCH = {1: 1, 4: 4, 16: 24}
NAT_ROWS = max(TILE // dil * pitch for dil, pitch in NAT_PITCH.items())


def _norm_streams_kernel(x_ref, g_ref, h0_ref, h1_ref, h2_ref, slab, padded):
    d = x_ref.shape[1]
    ssq = jnp.zeros((TILE, 1), F32)
    for c in range(d // LANES):
        xc = x_ref[:, c * LANES:(c + 1) * LANES]
        ssq = ssq + jnp.sum(xc * xc, axis=-1, keepdims=True)
    inv = lax.rsqrt(ssq / d + RMS_EPS)
    for c in range(d // LANES):
        cols = slice(c * LANES, (c + 1) * LANES)
        hc = x_ref[:, cols] * inv * g_ref[:, cols]
        h0_ref[:, cols] = hc.astype(BF16)
        slab[c % 2] = hc
        n = TILE // DILATIONS[1]
        for r in range(DILATIONS[1]):
            h1_ref[r * n:(r + 1) * n, cols] = slab[c % 2, pl.ds(r, n, stride=DILATIONS[1]), :].astype(BF16)
        dil, pitch = DILATIONS[2], NAT_PITCH[DILATIONS[2]]
        for l in range(TILE // dil):
            padded[c % 2, l * pitch:l * pitch + dil, :] = hc[l * dil:(l + 1) * dil]
        n = TILE // dil
        for r in range(dil):
            h2_ref[r * n:(r + 1) * n, cols] = padded[c % 2, pl.ds(r, n, stride=pitch), :].astype(BF16)


def _norm_streams(x, g):
    b, s, d = x.shape
    assert s % TILE == 0 and d % LANES == 0
    spec = pl.BlockSpec((None, TILE, d), lambda bi, i: (bi, i, 0))
    out = jax.ShapeDtypeStruct(x.shape, BF16)
    return pl.pallas_call(
        _norm_streams_kernel,
        grid=(b, s // TILE),
        in_specs=[spec, pl.BlockSpec((1, d), lambda bi, i: (0, 0))],
        out_specs=[spec, spec, spec],
        out_shape=[out, out, out],
        scratch_shapes=[pltpu.VMEM((2, TILE, LANES), F32), pltpu.VMEM((2, NAT_ROWS, LANES), F32)],
        compiler_params=_params("arbitrary", "arbitrary"),
        name="norm_streams",
    )(x, g)


def _read_nat(ref, g, pr, dil):
    pitch = NAT_PITCH[dil]
    if pitch == dil:
        return ref[g, pr, 0:TILE, :]
    return ref[g, pr, 0:TILE // dil * pitch, :].reshape(TILE // dil, pitch, LANES)[:, :dil, :].reshape(TILE, LANES)


def _dilated_kernel(q0, k0, v0, gate, q1, k1, v1, q2, k2, v2, y_ref,
                    ck0, cv0, ck1, cv1, ck2, cv2, out_nat, lse_nat, *, cw):
    i = pl.program_id(2)
    groups = ((q0, k0, v0, ck0, cv0), (q1, k1, v1, ck1, cv1), (q2, k2, v2, ck2, cv2))
    base, first = _band_bias(inclusive_far_edge=True, at_sequence_start=i == 0)

    @pl.when(i == 0)
    def _():
        for _, _, _, ck, cv in groups:
            ck[...] = jnp.zeros(ck.shape, BF16)
            cv[...] = jnp.zeros(cv.shape, BF16)

    for g, (dil, (q_ref, k_ref, v_ref, ck, cv)) in enumerate(zip(DILATIONS, groups)):
        n_sb = TILE // (dil * BAND)
        for r in range(dil):
            for sb in range(n_sb):
                row0 = (r * n_sb + sb) * BAND
                for pr in range(cw // PAIR):
                    lanes = slice(pr * PAIR, (pr + 1) * PAIR)
                    if sb == 0:
                        k_t = jnp.concatenate([ck[r * BAND:(r + 1) * BAND, lanes], k_ref[row0:row0 + BAND, lanes]], axis=0)
                        v_t = jnp.concatenate([cv[r * BAND:(r + 1) * BAND, lanes], v_ref[row0:row0 + BAND, lanes]], axis=0)
                    else:
                        k_t = k_ref[row0 - BAND:row0 + BAND, lanes]
                        v_t = v_ref[row0 - BAND:row0 + BAND, lanes]
                    (num, m, den), = _attend_pairs([q_ref[row0:row0 + BAND, lanes]], k_t, v_t,
                                                   first if sb == 0 else base)
                    pitch = NAT_PITCH[dil]
                    dst = pl.ds(sb * BAND * pitch + r, BAND, stride=pitch) if dil > 1 else pl.ds(sb * BAND, BAND)
                    out_nat[g, pr, dst, :] = num / den
                    lse_nat[g, pr, dst, :] = m + jnp.log2(den)
        for r in range(dil):
            last = ((r + 1) * n_sb - 1) * BAND
            ck[r * BAND:(r + 1) * BAND, :] = k_ref[last:last + BAND, :]
            cv[r * BAND:(r + 1) * BAND, :] = v_ref[last:last + BAND, :]

    for pr in range(cw // PAIR):
        lanes = slice(pr * PAIR, (pr + 1) * PAIR)
        (l0, o0), (l1, o1), (l2, o2) = [(_read_nat(lse_nat, g, pr, dil), _read_nat(out_nat, g, pr, dil))
                                        for g, dil in enumerate(DILATIONS)]
        mx = jnp.maximum(jnp.maximum(l0, l1), l2)
        e0, e1, e2 = jnp.exp2(l0 - mx), jnp.exp2(l1 - mx), jnp.exp2(l2 - mx)
        y = (e0 * o0 + e1 * o1 + e2 * o2) / (e0 + e1 + e2)
        y_ref[:, lanes] = (y * _silu(gate[:, lanes].astype(F32))).astype(BF16)


def _dilated_attention(p0, p1, p2, *, db, cw):
    b, s, _ = p0.shape
    assert s % TILE == 0 and db % cw == 0 and cw % PAIR == 0
    nc = db // cw
    blk = lambda part: pl.BlockSpec((None, TILE, cw), lambda bi, c, i: (bi, i, part * nc + c))
    carry = lambda dil: pltpu.VMEM((dil * BAND, cw), BF16)
    nat = pltpu.VMEM((len(DILATIONS), cw // PAIR, NAT_ROWS, LANES), F32)
    return pl.pallas_call(
        functools.partial(_dilated_kernel, cw=cw),
        grid=(b, nc, s // TILE),
        in_specs=[blk(0), blk(1), blk(2), blk(3), blk(0), blk(1), blk(2), blk(0), blk(1), blk(2)],
        out_specs=pl.BlockSpec((None, TILE, cw), lambda bi, c, i: (bi, i, c)),
        out_shape=jax.ShapeDtypeStruct((b, s, db), BF16),
        scratch_shapes=[carry(DILATIONS[0]), carry(DILATIONS[0]), carry(DILATIONS[1]), carry(DILATIONS[1]),
                        carry(DILATIONS[2]), carry(DILATIONS[2]), nat, nat],
        compiler_params=_params("arbitrary", "arbitrary", "arbitrary"),
        name="dilated_attention",
    )(p0, p0, p0, p0, p1, p1, p1, p2, p2, p2)


def kernel(x, norm_g, final_g, w_out, a_w_in, a_w_group, a_scale, b_w_in, b_sinks, c_w_in):
    b, s, d = x.shape
    depth = norm_g.shape[0]
    db = w_out.shape[1]
    qscale = HEAD_DIM ** -0.5 * LOG2E
    tq = min(512, s)
    tm = min(1024, b * s)
    fg = final_g.reshape(1, d)
    for i in range(depth):
        kind, j = i % N_MIXERS, i // N_MIXERS
        g = norm_g[i].reshape(1, d)
        wo = w_out[i].astype(BF16)
        last = i == depth - 1
        if kind == 0:
            x = _pool_layer(x, g, a_w_in[j].astype(BF16), a_w_group[j].astype(BF16),
                            a_scale[j].reshape(1, db), wo, fg, final_norm=last, tq=min(2 * tq, s), tsub=tq)
        elif kind == 1:
            w = b_w_in[j]
            kvw = (w.shape[1] - 2 * db) // 2
            wq, wk, wv, wg = jnp.split(w, [db, db + kvw, db + 2 * kvw], axis=1)
            w = jnp.concatenate([wq * qscale, wg, wk, wv], axis=1).astype(BF16)
            p = _proj(x.reshape(b * s, d), g, w, tm=tm, tsub=tm // 2)
            x = _swa_layer(x, p.reshape(b, s, -1), b_sinks[j], wo, lq=min(2 * tq, s))
        else:
            n_qkv = 3 * len(DILATIONS)
            part = jnp.arange(c_w_in.shape[2]) // db
            w = (c_w_in[j] * jnp.where((part % 3 == 0) & (part < n_qkv), qscale, 1.0)).astype(BF16)
            hs = _norm_streams(x, g)
            blocks = [(4, lambda jb: jnp.where(jb < 3, jb, n_qkv)), (3, lambda jb: jb + 3), (3, lambda jb: jb + 6)]
            ps = [_matmul(h.reshape(b * s, d), w, tm=tm, tn=db, n_blocks=nb, block_of=bof,
                          name=f"dil_proj{gi}").reshape(b, s, -1)
                  for gi, (h, (nb, bof)) in enumerate(zip(hs, blocks))]
            y = _dilated_attention(*ps, db=db, cw=2 * PAIR)
            x = _out_proj(x.reshape(b * s, d), y.reshape(b * s, db), wo, tm=tm).reshape(b, s, d)
        if last and kind != 0:
            raise NotImplementedError("final norm is fused into the pooling layer only")
    return x
```

```python
import functools

import jax
import jax.numpy as jnp
from jax import lax
from jax.experimental import pallas as pl
from jax.experimental.pallas import tpu as pltpu

HEAD_DIM = 64
PAIR = 2 * HEAD_DIM
RMS_EPS = 1e-5
BAND = 128
POOL_WINDOWS = (2, 4, 8, 16)
POOL_HALO = 16
SWA_GROUP = 8
DILATIONS = (1, 4, 16)
TILE = DILATIONS[-1] * BAND
N_MIXERS = 3
LANES = 128

V7X_VMEM_LIMIT_BYTES = 56 * 1024 * 1024
NEG = -1e30
LOG2E = 1.4426950408889634

F32 = jnp.float32
BF16 = jnp.bfloat16


def _params(*semantics):
    return pltpu.CompilerParams(dimension_semantics=semantics, vmem_limit_bytes=V7X_VMEM_LIMIT_BYTES)


def _rmsnorm(x, g):
    return x * lax.rsqrt(jnp.mean(x * x, axis=-1, keepdims=True) + RMS_EPS) * g


def _silu(g):
    return g / (1.0 + jnp.exp(-g))


def _band_bias(inclusive_far_edge, at_sequence_start):
    col = lax.broadcasted_iota(jnp.int32, (2 * BAND, BAND), 0)
    row = lax.broadcasted_iota(jnp.int32, (2 * BAND, BAND), 1)
    lo = row if inclusive_far_edge else row + 1
    valid = (col >= lo) & (col <= row + BAND)
    first = valid & (col >= jnp.where(at_sequence_start, BAND, 0))
    to_bias = lambda ok: jnp.where(ok, 0.0, NEG).astype(BF16)
    return to_bias(valid), to_bias(first)


def _row_selector():
    row = lax.broadcasted_iota(jnp.int32, (BAND, BAND), 0)
    col = lax.broadcasted_iota(jnp.int32, (BAND, BAND), 1)
    return (row == col).astype(F32).astype(BF16)


def _lane_halves():
    lane = lax.broadcasted_iota(jnp.int32, (1, PAIR), 1)
    first = (lane < HEAD_DIM).astype(F32)
    return first.astype(BF16), (1.0 - first).astype(BF16), lane < HEAD_DIM


def _attend_pairs(q_pairs, k_tile, v_tile, bias, sinks=None):
    n = len(q_pairs)
    za, zb, is_a = _lane_halves()
    sel = _row_selector()
    lhs = jnp.concatenate([jnp.concatenate([q * z, sel], axis=1) for q in q_pairs for z in (za, zb)], axis=0)
    rhs = jnp.concatenate([k_tile, bias], axis=1)
    s = lax.dot_general(lhs, rhs, (((1,), (1,)), ((), ())), preferred_element_type=F32)
    s = s.reshape(2 * n, BAND, 2 * BAND)
    m = jnp.max(s, axis=-1, keepdims=True)
    if sinks is not None:
        m = jnp.concatenate([jnp.maximum(m[h], sinks[h])[None] for h in range(2 * n)], axis=0)
    e = jnp.exp2(s - m).astype(BF16)
    ones = jnp.ones((2 * BAND, 1), BF16)
    v_aug = jnp.concatenate([jnp.concatenate([v_tile * za, ones * za], axis=1),
                             jnp.concatenate([v_tile * zb, ones * zb], axis=1)], axis=0)
    e_cat = jnp.concatenate([jnp.concatenate([e[2 * p], e[2 * p + 1]], axis=1) for p in range(n)], axis=0)
    acc = jnp.dot(e_cat, v_aug, preferred_element_type=F32)
    out = []
    for p in range(n):
        num = acc[p * BAND:(p + 1) * BAND, :PAIR]
        den = acc[p * BAND:(p + 1) * BAND, PAIR:]
        m_pair = jnp.where(is_a, m[2 * p], m[2 * p + 1])
        if sinks is not None:
            den = den + jnp.exp2(jnp.where(is_a, sinks[2 * p], sinks[2 * p + 1]) - m_pair)
        out.append((num, m_pair, den))
    return out


def _proj_kernel(x_ref, g_ref, w_ref, o_ref, *, tsub):
    for r0 in range(0, x_ref.shape[0], tsub):
        h = _rmsnorm(x_ref[r0:r0 + tsub, :], g_ref[...]).astype(BF16)
        o_ref[r0:r0 + tsub, :] = jnp.dot(h, w_ref[...], preferred_element_type=F32).astype(o_ref.dtype)


def _proj(x2d, g, w, *, tm, tsub):
    t, d = x2d.shape
    e = w.shape[1]
    assert t % tm == 0 and tm % tsub == 0
    return pl.pallas_call(
        functools.partial(_proj_kernel, tsub=tsub),
        grid=(t // tm,),
        in_specs=[
            pl.BlockSpec((tm, d), lambda i: (i, 0)),
            pl.BlockSpec((1, d), lambda i: (0, 0)),
            pl.BlockSpec((d, e), lambda i: (0, 0), pipeline_mode=pl.Buffered(1)),
        ],
        out_specs=pl.BlockSpec((tm, e), lambda i: (i, 0)),
        out_shape=jax.ShapeDtypeStruct((t, e), BF16),
        compiler_params=_params("arbitrary"),
        name="in_proj",
    )(x2d, g, w)


def _matmul_kernel(a_ref, w_ref, o_ref):
    o_ref[...] = jnp.dot(a_ref[...], w_ref[...], preferred_element_type=F32).astype(o_ref.dtype)


def _matmul(a, w, *, tm, tn, n_blocks, block_of, name):
    t, d = a.shape
    assert t % tm == 0 and w.shape[1] % tn == 0
    return pl.pallas_call(
        _matmul_kernel,
        grid=(t // tm, n_blocks),
        in_specs=[pl.BlockSpec((tm, d), lambda i, j: (i, 0)), pl.BlockSpec((d, tn), lambda i, j: (0, block_of(j)))],
        out_specs=pl.BlockSpec((tm, tn), lambda i, j: (i, j)),
        out_shape=jax.ShapeDtypeStruct((t, n_blocks * tn), BF16),
        compiler_params=_params("arbitrary", "arbitrary"),
        name=name,
    )(a, w)


def _out_proj_kernel(x_ref, y_ref, w_ref, o_ref):
    o_ref[...] = x_ref[...] + jnp.dot(y_ref[...], w_ref[...], preferred_element_type=F32)


def _out_proj(x2d, y2d, w, *, tm):
    t, d = x2d.shape
    assert t % tm == 0
    return pl.pallas_call(
        _out_proj_kernel,
        grid=(t // tm,),
        in_specs=[pl.BlockSpec((tm, d), lambda i: (i, 0)),
                  pl.BlockSpec((tm, y2d.shape[1]), lambda i: (i, 0)),
                  pl.BlockSpec(w.shape, lambda i: (0, 0))],
        out_specs=pl.BlockSpec((tm, d), lambda i: (i, 0)),
        out_shape=jax.ShapeDtypeStruct(x2d.shape, F32),
        compiler_params=_params("arbitrary"),
        name="out_proj",
    )(x2d, y2d, w)


def _pool_layer_kernel(x_ref, g_ref, win_ref, wg_ref, sc_ref, wout_ref, fg_ref, o_ref, ubuf, zbuf,
                       *, tq, tsub, db, final_norm):
    j = pl.program_id(1)

    @pl.when(j == 0)
    def _():
        ubuf[0:POOL_HALO, :] = jnp.zeros((POOL_HALO, db), F32)

    @pl.when(j > 0)
    def _():
        ubuf[0:POOL_HALO, :] = ubuf[tq:tq + POOL_HALO, :]

    gd = db // len(POOL_WINDOWS)
    for r0 in range(0, tq, tsub):
        rows = slice(r0, r0 + tsub)
        x = x_ref[rows, :]
        h = _rmsnorm(x, g_ref[...]).astype(BF16)
        p = jnp.dot(h, win_ref[...], preferred_element_type=F32)
        ubuf[POOL_HALO + r0:POOL_HALO + r0 + tsub, :] = p[:, :db]
        t_abs = j * tq + r0 + lax.broadcasted_iota(jnp.int32, (tsub, 1), 0)
        for gi, w in enumerate(POOL_WINDOWS):
            cols = slice(gi * gd, (gi + 1) * gd)
            ext = ubuf[r0:r0 + POOL_HALO + tsub, cols]
            acc, span = ext, 1
            while span < w:
                acc = acc + pltpu.roll(acc, span, 0)
                span *= 2
            u, acc = ext[POOL_HALO:], acc[POOL_HALO:]
            count = jnp.minimum(t_abs + 1, w).astype(F32)
            d = (acc / count - u).astype(BF16)
            y = jnp.dot(d, wg_ref[gi], preferred_element_type=F32) * sc_ref[:, cols]
            gate = p[:, db + gi * gd:db + (gi + 1) * gd]
            zbuf[rows, cols] = (y * _silu(gate)).astype(BF16)
        out = x + jnp.dot(zbuf[rows, :], wout_ref[...], preferred_element_type=F32)
        if final_norm:
            out = _rmsnorm(out, fg_ref[...])
        o_ref[rows, :] = out


def _pool_layer(x, g, w_in, w_group, scale, w_out, final_g, *, final_norm, tq, tsub):
    b, s, d = x.shape
    db = w_out.shape[0]
    assert s % tq == 0 and tq % tsub == 0 and tsub >= POOL_HALO
    const2 = lambda bi, j: (0, 0)
    return pl.pallas_call(
        functools.partial(_pool_layer_kernel, tq=tq, tsub=tsub, db=db, final_norm=final_norm),
        grid=(b, s // tq),
        in_specs=[
            pl.BlockSpec((None, tq, d), lambda bi, j: (bi, j, 0)),
            pl.BlockSpec((1, d), const2),
            pl.BlockSpec(w_in.shape, const2, pipeline_mode=pl.Buffered(1)),
            pl.BlockSpec(w_group.shape, lambda bi, j: (0, 0, 0), pipeline_mode=pl.Buffered(1)),
            pl.BlockSpec((1, db), const2),
            pl.BlockSpec(w_out.shape, const2, pipeline_mode=pl.Buffered(1)),
            pl.BlockSpec((1, d), const2),
        ],
        out_specs=pl.BlockSpec((None, tq, d), lambda bi, j: (bi, j, 0)),
        out_shape=jax.ShapeDtypeStruct(x.shape, F32),
        scratch_shapes=[pltpu.VMEM((POOL_HALO + tq, db), F32), pltpu.VMEM((tq, db), BF16)],
        compiler_params=_params("arbitrary", "arbitrary"),
        name="pool_layer",
    )(x, g, w_in, w_group, scale, w_out, final_g)


def _swa_kernel(sink_ref, q_ref, gate_ref, kvp_ref, kvc_ref, x_ref, wout_ref, o_ref,
                ybuf, kdup, vdup, *, lq, n_heads):
    l = pl.program_id(1)
    base, first = _band_bias(inclusive_far_edge=False, at_sequence_start=l == 0)
    n_kv = n_heads // SWA_GROUP
    pairs_per_kv = SWA_GROUP // 2
    for j in range(n_kv):
        for off, dst in ((0, kdup), (n_kv * HEAD_DIM, vdup)):
            cols = slice(off + j * HEAD_DIM, off + (j + 1) * HEAD_DIM)
            dst[j, 0:BAND, :] = jnp.concatenate([kvp_ref[:, cols]] * 2, axis=1)
            dst[j, BAND:BAND + lq, :] = jnp.concatenate([kvc_ref[:, cols]] * 2, axis=1)
    for sb in range(lq // BAND):
        rows = slice(sb * BAND, (sb + 1) * BAND)
        bias = first if sb == 0 else base
        for j in range(n_kv):
            pair_cols = [slice((j * pairs_per_kv + p) * PAIR, (j * pairs_per_kv + p + 1) * PAIR)
                         for p in range(pairs_per_kv)]
            for p, c in enumerate(pair_cols):
                (num, _, den), = _attend_pairs(
                    [q_ref[rows, c]], kdup[j, sb * BAND:(sb + 2) * BAND, :], vdup[j, sb * BAND:(sb + 2) * BAND, :],
                    bias, [sink_ref[j * SWA_GROUP + 2 * p + h] * LOG2E for h in range(2)])
                ybuf[rows, c] = (num / den * _silu(gate_ref[rows, c].astype(F32))).astype(BF16)
    o_ref[...] = x_ref[...] + jnp.dot(ybuf[...], wout_ref[...], preferred_element_type=F32)


def _swa_layer(x, p, sinks, w_out, *, lq):
    b, s, d = x.shape
    db = w_out.shape[0]
    n_heads = db // HEAD_DIM
    n_kv = n_heads // SWA_GROUP
    kv = n_kv * HEAD_DIM
    assert s % lq == 0 and lq % BAND == 0 and (2 * db) % (2 * kv) == 0
    kvcol = 2 * db // (2 * kv)
    prev = lambda bi, l: jnp.maximum(l * (lq // BAND) - 1, 0)
    return pl.pallas_call(
        functools.partial(_swa_kernel, lq=lq, n_heads=n_heads),
        grid=(b, s // lq),
        in_specs=[
            pl.BlockSpec(memory_space=pltpu.SMEM),
            pl.BlockSpec((None, lq, db), lambda bi, l: (bi, l, 0)),
            pl.BlockSpec((None, lq, db), lambda bi, l: (bi, l, 1)),
            pl.BlockSpec((None, BAND, 2 * kv), lambda bi, l: (bi, prev(bi, l), kvcol)),
            pl.BlockSpec((None, lq, 2 * kv), lambda bi, l: (bi, l, kvcol)),
            pl.BlockSpec((None, lq, d), lambda bi, l: (bi, l, 0)),
            pl.BlockSpec(w_out.shape, lambda bi, l: (0, 0)),
        ],
        out_specs=pl.BlockSpec((None, lq, d), lambda bi, l: (bi, l, 0)),
        out_shape=jax.ShapeDtypeStruct(x.shape, F32),
        scratch_shapes=[pltpu.VMEM((lq, db), BF16),
                        pltpu.VMEM((n_kv, BAND + lq, PAIR), BF16),
                        pltpu.VMEM((n_kv, BAND + lq, PAIR), BF16)],
        compiler_params=_params("arbitrary", "arbitrary"),
        name="swa_layer",
    )(sinks, p, p, p, p, x, w_out)


NAT_PITCH = {1: 1, 4: 4, 16: 24}
NAT_ROWS = max(TILE // dil * pitch for dil, pitch in NAT_PITCH.items())


def _norm_streams_kernel(x_ref, g_ref, h0_ref, h1_ref, h2_ref, slab, padded):
    d = x_ref.shape[1]
    ssq = jnp.zeros((TILE, 1), F32)
    for c in range(d // LANES):
        xc = x_ref[:, c * LANES:(c + 1) * LANES]
        ssq = ssq + jnp.sum(xc * xc, axis=-1, keepdims=True)
    inv = lax.rsqrt(ssq / d + RMS_EPS)
    for c in range(d // LANES):
        cols = slice(c * LANES, (c + 1) * LANES)
        hc = x_ref[:, cols] * inv * g_ref[:, cols]
        h0_ref[:, cols] = hc.astype(BF16)
        slab[c % 2] = hc
        n = TILE // DILATIONS[1]
        for r in range(DILATIONS[1]):
            h1_ref[r * n:(r + 1) * n, cols] = slab[c % 2, pl.ds(r, n, stride=DILATIONS[1]), :].astype(BF16)
        dil, pitch = DILATIONS[2], NAT_PITCH[DILATIONS[2]]
        for l in range(TILE // dil):
            padded[c % 2, l * pitch:l * pitch + dil, :] = hc[l * dil:(l + 1) * dil]
        n = TILE // dil
        for r in range(dil):
            h2_ref[r * n:(r + 1) * n, cols] = padded[c % 2, pl.ds(r, n, stride=pitch), :].astype(BF16)


def _norm_streams(x, g):
    b, s, d = x.shape
    assert s % TILE == 0 and d % LANES == 0
    spec = pl.BlockSpec((None, TILE, d), lambda bi, i: (bi, i, 0))
    out = jax.ShapeDtypeStruct(x.shape, BF16)
    return pl.pallas_call(
        _norm_streams_kernel,
        grid=(b, s // TILE),
        in_specs=[spec, pl.BlockSpec((1, d), lambda bi, i: (0, 0))],
        out_specs=[spec, spec, spec],
        out_shape=[out, out, out],
        scratch_shapes=[pltpu.VMEM((2, TILE, LANES), F32), pltpu.VMEM((2, NAT_ROWS, LANES), F32)],
        compiler_params=_params("arbitrary", "arbitrary"),
        name="norm_streams",
    )(x, g)


def _read_nat(ref, g, pr, dil):
    pitch = NAT_PITCH[dil]
    if pitch == dil:
        return ref[g, pr, 0:TILE, :]
    return ref[g, pr, 0:TILE // dil * pitch, :].reshape(TILE // dil, pitch, LANES)[:, :dil, :].reshape(TILE, LANES)


def _dilated_kernel(q0, k0, v0, gate, q1, k1, v1, q2, k2, v2, y_ref,
                    ck0, cv0, ck1, cv1, ck2, cv2, out_nat, lse_nat, *, cw):
    i = pl.program_id(2)
    groups = ((q0, k0, v0, ck0, cv0), (q1, k1, v1, ck1, cv1), (q2, k2, v2, ck2, cv2))
    base, first = _band_bias(inclusive_far_edge=True, at_sequence_start=i == 0)

    @pl.when(i == 0)
    def _():
        for _, _, _, ck, cv in groups:
            ck[...] = jnp.zeros(ck.shape, BF16)
            cv[...] = jnp.zeros(cv.shape, BF16)

    for g, (dil, (q_ref, k_ref, v_ref, ck, cv)) in enumerate(zip(DILATIONS, groups)):
        n_sb = TILE // (dil * BAND)
        for r in range(dil):
            for sb in range(n_sb):
                row0 = (r * n_sb + sb) * BAND
                for pr in range(cw // PAIR):
                    lanes = slice(pr * PAIR, (pr + 1) * PAIR)
                    if sb == 0:
                        k_t = jnp.concatenate([ck[r * BAND:(r + 1) * BAND, lanes], k_ref[row0:row0 + BAND, lanes]], axis=0)
                        v_t = jnp.concatenate([cv[r * BAND:(r + 1) * BAND, lanes], v_ref[row0:row0 + BAND, lanes]], axis=0)
                    else:
                        k_t = k_ref[row0 - BAND:row0 + BAND, lanes]
                        v_t = v_ref[row0 - BAND:row0 + BAND, lanes]
                    (num, m, den), = _attend_pairs([q_ref[row0:row0 + BAND, lanes]], k_t, v_t,
                                                   first if sb == 0 else base)
                    pitch = NAT_PITCH[dil]
                    dst = pl.ds(sb * BAND * pitch + r, BAND, stride=pitch) if dil > 1 else pl.ds(sb * BAND, BAND)
                    out_nat[g, pr, dst, :] = num / den
                    lse_nat[g, pr, dst, :] = m + jnp.log2(den)
        for r in range(dil):
            last = ((r + 1) * n_sb - 1) * BAND
            ck[r * BAND:(r + 1) * BAND, :] = k_ref[last:last + BAND, :]
            cv[r * BAND:(r + 1) * BAND, :] = v_ref[last:last + BAND, :]

    for pr in range(cw // PAIR):
        lanes = slice(pr * PAIR, (pr + 1) * PAIR)
        (l0, o0), (l1, o1), (l2, o2) = [(_read_nat(lse_nat, g, pr, dil), _read_nat(out_nat, g, pr, dil))
                                        for g, dil in enumerate(DILATIONS)]
        mx = jnp.maximum(jnp.maximum(l0, l1), l2)
        e0, e1, e2 = jnp.exp2(l0 - mx), jnp.exp2(l1 - mx), jnp.exp2(l2 - mx)
        y = (e0 * o0 + e1 * o1 + e2 * o2) / (e0 + e1 + e2)
        y_ref[:, lanes] = (y * _silu(gate[:, lanes].astype(F32))).astype(BF16)


def _dilated_attention(p0, p1, p2, *, db, cw):
    b, s, _ = p0.shape
    assert s % TILE == 0 and db % cw == 0 and cw % PAIR == 0
    nc = db // cw
    blk = lambda part: pl.BlockSpec((None, TILE, cw), lambda bi, c, i: (bi, i, part * nc + c))
    carry = lambda dil: pltpu.VMEM((dil * BAND, cw), BF16)
    nat = pltpu.VMEM((len(DILATIONS), cw // PAIR, NAT_ROWS, LANES), F32)
    return pl.pallas_call(
        functools.partial(_dilated_kernel, cw=cw),
        grid=(b, nc, s // TILE),
        in_specs=[blk(0), blk(1), blk(2), blk(3), blk(0), blk(1), blk(2), blk(0), blk(1), blk(2)],
        out_specs=pl.BlockSpec((None, TILE, cw), lambda bi, c, i: (bi, i, c)),
        out_shape=jax.ShapeDtypeStruct((b, s, db), BF16),
        scratch_shapes=[carry(DILATIONS[0]), carry(DILATIONS[0]), carry(DILATIONS[1]), carry(DILATIONS[1]),
                        carry(DILATIONS[2]), carry(DILATIONS[2]), nat, nat],
        compiler_params=_params("arbitrary", "arbitrary", "arbitrary"),
        name="dilated_attention",
    )(p0, p0, p0, p0, p1, p1, p1, p2, p2, p2)


def kernel(x, norm_g, final_g, w_out, a_w_in, a_w_group, a_scale, b_w_in, b_sinks, c_w_in):
    b, s, d = x.shape
    depth = norm_g.shape[0]
    db = w_out.shape[1]
    qscale = HEAD_DIM ** -0.5 * LOG2E
    tq = min(512, s)
    tm = min(1024, b * s)
    fg = final_g.reshape(1, d)
    for i in range(depth):
        kind, j = i % N_MIXERS, i // N_MIXERS
        g = norm_g[i].reshape(1, d)
        wo = w_out[i].astype(BF16)
        last = i == depth - 1
        if kind == 0:
            x = _pool_layer(x, g, a_w_in[j].astype(BF16), a_w_group[j].astype(BF16),
                            a_scale[j].reshape(1, db), wo, fg, final_norm=last, tq=min(2 * tq, s), tsub=tq)
        elif kind == 1:
            w = b_w_in[j]
            kvw = (w.shape[1] - 2 * db) // 2
            wq, wk, wv, wg = jnp.split(w, [db, db + kvw, db + 2 * kvw], axis=1)
            w = jnp.concatenate([wq * qscale, wg, wk, wv], axis=1).astype(BF16)
            p = _proj(x.reshape(b * s, d), g, w, tm=tm, tsub=tm // 2)
            x = _swa_layer(x, p.reshape(b, s, -1), b_sinks[j], wo, lq=min(2 * tq, s))
        else:
            n_qkv = 3 * len(DILATIONS)
            part = jnp.arange(c_w_in.shape[2]) // db
            w = (c_w_in[j] * jnp.where((part % 3 == 0) & (part < n_qkv), qscale, 1.0)).astype(BF16)
            hs = _norm_streams(x, g)
            blocks = [(4, lambda jb: jnp.where(jb < 3, jb, n_qkv)), (3, lambda jb: jb + 3), (3, lambda jb: jb + 6)]
            ps = [_matmul(h.reshape(b * s, d), w, tm=tm, tn=db, n_blocks=nb, block_of=bof,
                          name=f"dil_proj{gi}").reshape(b, s, -1)
                  for gi, (h, (nb, bof)) in enumerate(zip(hs, blocks))]
            y = _dilated_attention(*ps, db=db, cw=2 * PAIR)
            x = _out_proj(x.reshape(b * s, d), y.reshape(b * s, db), wo, tm=tm).reshape(b, s, d)
        if last and kind != 0:
            raise NotImplementedError("final norm is fused into the pooling layer only")
    return x
```
